```python
import jax, jax.numpy as jnp
from jax import lax
import numpy as np

D_MODEL = 1024
BATCH = 8
SEQ = 2048
DEPTH = 4

CHUNK = 64
N_MIXERS = 2
N_RET_LAYERS = (DEPTH + 1) // 2
N_SB_LAYERS = DEPTH // 2
RET_HEADS = D_MODEL // 256
RET_QK_DIM = D_MODEL // RET_HEADS
RET_V_DIM = 2 * RET_QK_DIM
RET_IN_DIM = 2 * RET_HEADS * RET_QK_DIM + 2 * RET_HEADS * RET_V_DIM
ROPE_BASE = 10000.0
GN_EPS = 1e-6
SB_HEADS = 16
SB_HEAD_DIM = D_MODEL // SB_HEADS
Q_BLOCK = 128
N_EXPERTS = 32
TOP_K = 4
D_EXPERT = D_MODEL
SWIGLU_LIMIT = 7.0
SWIGLU_ALPHA = 1.702
MOE_BLOCK = 128
D_PLE = 256
DN_ALPHA = float((2 * DEPTH) ** 0.25)
DN_BETA = float((8 * DEPTH) ** -0.25)
LN_EPS = 1e-5

kernel_name = 'hybrid_retention_stickbreaking_moe_encoder'


def layer_norm(x, g, b):
    xf = x.astype(jnp.float32)
    mu = jnp.mean(xf, axis=-1, keepdims=True)
    var = jnp.mean(jnp.square(xf - mu), axis=-1, keepdims=True)
    return ((xf - mu) * lax.rsqrt(var + LN_EPS)).astype(x.dtype) * g + b


def rotary(x, pos):
    half = x.shape[-1] // 2
    inv_freq = 1.0 / (ROPE_BASE ** (jnp.arange(half, dtype=jnp.float32) / half))
    ang = pos.astype(jnp.float32)[:, None] * inv_freq[None, :]
    cos = jnp.cos(ang)[None, :, None, :].astype(x.dtype)
    sin = jnp.sin(ang)[None, :, None, :].astype(x.dtype)
    x1, x2 = x[..., :half], x[..., half:]
    return jnp.concatenate([x1 * cos - x2 * sin, x1 * sin + x2 * cos], axis=-1)


def retention_mixer(x, w_in, w_out):
    B, S, _ = x.shape
    H, dk, dv = RET_HEADS, RET_QK_DIM, RET_V_DIM
    proj = x @ w_in
    q, k, v, g = jnp.split(proj, [H * dk, 2 * H * dk, 2 * H * dk + H * dv], axis=-1)
    pos = jnp.arange(S)
    q = rotary(q.reshape(B, S, H, dk), pos)
    k = rotary(k.reshape(B, S, H, dk), pos) * (dk ** -0.5)
    v = v.reshape(B, S, H, dv)
    n_chunks = S // CHUNK
    log_gamma = jnp.log(1.0 - 2.0 ** (-5.0 - jnp.arange(H, dtype=jnp.float32)))
    idx = jnp.arange(CHUNK, dtype=jnp.float32)
    inner_decay = jnp.exp(log_gamma[:, None, None] * jnp.abs(idx[:, None] - idx[None, :]))
    query_decay = jnp.exp(log_gamma[:, None] * (idx + 1.0))
    key_decay = jnp.exp(log_gamma[:, None] * (CHUNK - 1.0 - idx))
    chunk_decay = jnp.exp(log_gamma * CHUNK)

    def to_chunks(t):
        return t.reshape(B, n_chunks, CHUNK, H, t.shape[-1]).transpose(1, 0, 3, 2, 4)

    def step(state, inp):
        qi, ki, vi = inp
        scores = jnp.einsum('bhid,bhjd->bhij', qi, ki) * inner_decay[None]
        inner = jnp.einsum('bhij,bhje->bhie', scores, vi)
        cross = jnp.einsum('bhid,bhde->bhie', qi, state) * query_decay[None, :, :, None]
        new_state = state * chunk_decay[None, :, None, None] + jnp.einsum(
            'bhjd,bhje->bhde', ki * key_decay[None, :, :, None], vi)
        return new_state, (inner + cross).astype(jnp.float32)

    state0 = jnp.zeros((B, H, dk, dv), jnp.float32)
    _, out = lax.scan(step, state0, (to_chunks(q), to_chunks(k), to_chunks(v)))
    out = out.transpose(1, 0, 3, 2, 4).reshape(B, S, H, dv)
    mu = jnp.mean(out, axis=-1, keepdims=True)
    var = jnp.mean(jnp.square(out - mu), axis=-1, keepdims=True)
    normed = ((out - mu) * lax.rsqrt(var + GN_EPS)).reshape(B, S, H * dv).astype(g.dtype)
    return (jax.nn.silu(g) * normed) @ w_out


def stick_breaking_mixer(x, w_in, w_out):
    B, S, _ = x.shape
    H, d = SB_HEADS, SB_HEAD_DIM
    q, k, v = jnp.split(x @ w_in, 3, axis=-1)
    q = q.reshape(B, S, H, d)
    k = k.reshape(B, S, H, d)
    v = v.reshape(B, S, H, d)
    outs = []
    for blk in range(S // Q_BLOCK):
        q0 = blk * Q_BLOCK
        kv_end = q0 + Q_BLOCK
        qb, kb, vb = q[:, q0:kv_end], k[:, :kv_end], v[:, :kv_end]
        z = jnp.einsum('bqhd,bkhd->bhqk', qb, kb).astype(jnp.float32) * (d ** -0.5)
        t_idx = q0 + jnp.arange(Q_BLOCK)[:, None]
        s_idx = jnp.arange(kv_end)[None, :]
        past = s_idx < t_idx
        log_not = jnp.where(past, -jax.nn.softplus(z), 0.0)
        later = lax.cumsum(log_not, axis=3, reverse=True) - log_not
        a = jnp.where(past, jnp.exp(jax.nn.log_sigmoid(z) + later), 0.0)
        outs.append(jnp.einsum('bhqk,bkhd->bqhd', a.astype(vb.dtype), vb))
    o = jnp.concatenate(outs, axis=1).reshape(B, S, H * d)
    return o @ w_out


def moe_channel_mixer(x, router_w, router_b, w_gate_up, b_gate_up, w_down, b_down):
    B, S, D = x.shape
    h = x.reshape(B * S, D)
    T = h.shape[0]
    logits = (h @ router_w + router_b).astype(jnp.float32)
    top_logits, top_idx = lax.top_k(logits, TOP_K)
    gates = jax.nn.softmax(top_logits, axis=-1)
    flat_e = top_idx.reshape(-1)
    flat_tok = jnp.repeat(jnp.arange(T, dtype=jnp.int32), TOP_K)
    flat_gate = gates.reshape(-1)
    order = jnp.argsort(flat_e)
    sorted_e = flat_e[order]
    counts = jnp.bincount(flat_e, length=N_EXPERTS)
    padded = (counts + MOE_BLOCK - 1) // MOE_BLOCK * MOE_BLOCK
    pad_ends = jnp.cumsum(padded)
    pad_starts = pad_ends - padded
    starts = jnp.cumsum(counts) - counts
    dest = pad_starts[sorted_e] + jnp.arange(T * TOP_K) - starts[sorted_e]
    n_blocks = -(-(T * TOP_K + N_EXPERTS * (MOE_BLOCK - 1)) // MOE_BLOCK)
    n_rows = n_blocks * MOE_BLOCK
    row_tok = jnp.full((n_rows,), T, jnp.int32).at[dest].set(flat_tok[order])
    row_gate = jnp.zeros((n_rows,), jnp.float32).at[dest].set(flat_gate[order])
    block_start = jnp.arange(n_blocks) * MOE_BLOCK
    block_expert = jnp.minimum(jnp.searchsorted(pad_ends, block_start, side='right'), N_EXPERTS - 1)
    h_pad = jnp.concatenate([h, jnp.zeros((1, D), h.dtype)], axis=0)
    xs = h_pad[row_tok].reshape(n_blocks, MOE_BLOCK, D)

    def expert_block(args):
        xb, e = args
        gu = xb @ w_gate_up[e] + b_gate_up[e]
        gate = jnp.minimum(gu[:, 0::2], SWIGLU_LIMIT)
        up = jnp.clip(gu[:, 1::2], -SWIGLU_LIMIT, SWIGLU_LIMIT)
        act = (up + 1.0) * (gate * jax.nn.sigmoid(gate * SWIGLU_ALPHA))
        return act @ w_down[e] + b_down[e]

    ys = lax.map(expert_block, (xs, block_expert))
    ys = ys.reshape(n_rows, D) * row_gate[:, None].astype(ys.dtype)
    out = jnp.zeros((T + 1, D), ys.dtype).at[row_tok].add(ys)[:T]
    return out.reshape(B, S, D)


def setup_inputs(seed: int = 0) -> dict:
    key = jax.random.key(seed)
    ks = jax.random.split(key, 20)
    f32 = jnp.float32
    D, F, E = D_MODEL, D_EXPERT, N_EXPERTS
    nrm = lambda k, shape: jax.random.normal(k, shape, f32)
    x = nrm(ks[0], (BATCH, SEQ, D))
    p = nrm(ks[1], (DEPTH, BATCH, SEQ, D_PLE))
    h_qk, h_v = RET_HEADS * RET_QK_DIM, RET_HEADS * RET_V_DIM
    ret_col_scale = jnp.concatenate([jnp.ones((2 * h_qk,), f32), jnp.full((h_v,), DN_BETA, f32), jnp.ones((h_v,), f32)])
    ret_w_in = nrm(ks[2], (N_RET_LAYERS, D, RET_IN_DIM)) * (D ** -0.5) * ret_col_scale
    ret_w_out = nrm(ks[3], (N_RET_LAYERS, h_v, D)) * (h_v ** -0.5) * DN_BETA
    sb_col_scale = jnp.concatenate([jnp.ones((2 * D,), f32), jnp.full((D,), DN_BETA, f32)])
    sb_w_in = nrm(ks[4], (N_SB_LAYERS, D, 3 * D)) * (D ** -0.5) * sb_col_scale
    sb_w_out = nrm(ks[5], (N_SB_LAYERS, D, D)) * (D ** -0.5) * DN_BETA
    ln1_g = 1.0 + 0.02 * nrm(ks[6], (DEPTH, D))
    ln1_b = 0.02 * nrm(ks[7], (DEPTH, D))
    router_w = nrm(ks[8], (DEPTH, D, E)) * (D ** -0.5)
    router_b = 0.01 * nrm(ks[9], (DEPTH, E))
    w_gate_up = nrm(ks[10], (DEPTH, E, D, 2 * F)) * (D ** -0.5)
    b_gate_up = 0.02 * nrm(ks[11], (DEPTH, E, 2 * F))
    w_down = nrm(ks[12], (DEPTH, E, F, D)) * (F ** -0.5) * DN_BETA
    b_down = 0.02 * nrm(ks[13], (DEPTH, E, D))
    ln2_g = 1.0 + 0.02 * nrm(ks[14], (DEPTH, D))
    ln2_b = 0.02 * nrm(ks[15], (DEPTH, D))
    ple_w = nrm(ks[16], (DEPTH, D_PLE, D)) * (D_PLE ** -0.5) * DN_BETA
    ple_gate_w = nrm(ks[17], (DEPTH, D, D)) * (D ** -0.5)
    ple_gate_b = 0.02 * nrm(ks[18], (DEPTH, D))
    return {'x': x, 'p': p, 'ret_w_in': ret_w_in, 'ret_w_out': ret_w_out,
            'sb_w_in': sb_w_in, 'sb_w_out': sb_w_out, 'ln1_g': ln1_g, 'ln1_b': ln1_b,
            'router_w': router_w, 'router_b': router_b, 'w_gate_up': w_gate_up,
            'b_gate_up': b_gate_up, 'w_down': w_down, 'b_down': b_down,
            'ln2_g': ln2_g, 'ln2_b': ln2_b, 'ple_w': ple_w, 'ple_gate_w': ple_gate_w,
            'ple_gate_b': ple_gate_b}


def reference(x, p, ret_w_in, ret_w_out, sb_w_in, sb_w_out, ln1_g, ln1_b, router_w, router_b,
              w_gate_up, b_gate_up, w_down, b_down, ln2_g, ln2_b, ple_w, ple_gate_w, ple_gate_b):
    for i in range(DEPTH):
        j = i // N_MIXERS
        if i % N_MIXERS == 0:
            mix = retention_mixer(x, ret_w_in[j], ret_w_out[j])
        else:
            mix = stick_breaking_mixer(x, sb_w_in[j], sb_w_out[j])
        x = layer_norm(DN_ALPHA * x + mix, ln1_g[i], ln1_b[i])
        ffn = moe_channel_mixer(x, router_w[i], router_b[i], w_gate_up[i], b_gate_up[i],
                                w_down[i], b_down[i])
        x = layer_norm(DN_ALPHA * x + ffn, ln2_g[i], ln2_b[i])
        ple_gate = jax.nn.sigmoid(x @ ple_gate_w[i] + ple_gate_b[i])
        x = x + ple_gate * (p[i] @ ple_w[i])
    return x
```

```python
import functools

import jax
import jax.numpy as jnp
import numpy as np
from jax import lax
from jax.experimental import pallas as pl
from jax.experimental.pallas import tpu as pltpu

F32 = jnp.float32
BF16 = jnp.bfloat16

RET_QK_DIM = 256
RET_V_DIM = 512
ROPE_BASE = 10000.0
GN_EPS = 1e-6
LN_EPS = 1e-5
SB_HEADS = 16
TOP_K = 4
SWIGLU_LIMIT = 7.0
SWIGLU_ALPHA = 1.702

LANES = 128
MXU_DIM = 256
VMEM_LIMIT = 56 * 1024 * 1024

RET_BLOCK = 256
RET_CHUNK = 64
SB_TILE = 256
ROW_BLOCK = 256
EXP_ZERO_BELOW = -104.0


def _cparams(sem):
    return pltpu.CompilerParams(dimension_semantics=sem, vmem_limit_bytes=VMEM_LIMIT)


def _mm_kernel(x_ref, w_ref, o_ref):
    o_ref[...] = jnp.dot(x_ref[...], w_ref[...], preferred_element_type=F32).astype(o_ref.dtype)


def _matmul(x, w, out_dtype, tm=1024, tn=1024):
    m, k = x.shape
    n = w.shape[1]
    tm = min(tm, m)
    tn = min(tn, n)
    return pl.pallas_call(
        _mm_kernel,
        grid=(n // tn, m // tm),
        in_specs=[pl.BlockSpec((tm, k), lambda j, i: (i, 0)),
                  pl.BlockSpec((k, tn), lambda j, i: (0, j))],
        out_specs=pl.BlockSpec((tm, tn), lambda j, i: (i, j)),
        out_shape=jax.ShapeDtypeStruct((m, n), out_dtype),
        compiler_params=_cparams(("parallel", "parallel")),
        name="proj_matmul",
    )(x, w)


def _retention_kernel(q_ref, k_ref, v_ref, g_ref, cos_ref, sin_ref, dmat_ref, qdec_ref, kdec_ref,
                      cdec_ref, o_ref, state_ref):
    half = RET_QK_DIM // 2
    cos = cos_ref[...]
    sin = sin_ref[...]

    def rot(t):
        t1, t2 = t[:, :half], t[:, half:]
        return jnp.concatenate([t1 * cos - t2 * sin, t1 * sin + t2 * cos], axis=-1)

    q = rot(q_ref[...].astype(F32))
    k = rot(k_ref[...].astype(F32)) * (RET_QK_DIM ** -0.5)
    v = v_ref[...]

    @pl.when(pl.program_id(2) == 0)
    def _():
        state_ref[...] = jnp.zeros_like(state_ref)

    state = state_ref[...]
    scores = lax.dot_general(q.astype(BF16), k.astype(BF16), (((1,), (1,)), ((), ())),
                             preferred_element_type=F32) * dmat_ref[...]
    inner = jnp.dot(scores.astype(BF16), v, preferred_element_type=F32)
    cross = jnp.dot((q * qdec_ref[...]).astype(BF16), state.astype(BF16), preferred_element_type=F32)
    kv = lax.dot_general((k * kdec_ref[...]).astype(BF16), v, (((0,), (0,)), ((), ())),
                         preferred_element_type=F32)
    state_ref[...] = state * cdec_ref[...] + kv

    out = inner + cross
    mu = jnp.mean(out, axis=-1, keepdims=True)
    var = jnp.mean(jnp.square(out - mu), axis=-1, keepdims=True)
    normed = (out - mu) * lax.rsqrt(var + GN_EPS)
    g = g_ref[...].astype(F32)
    o_ref[...] = (g * jax.nn.sigmoid(g) * normed).astype(o_ref.dtype)


def _retention_tables(seq, heads):
    half = RET_QK_DIM // 2
    inv_freq = 1.0 / (ROPE_BASE ** (jnp.arange(half, dtype=F32) / half))
    ang = jnp.arange(seq, dtype=F32)[:, None] * inv_freq[None, :]
    log_gamma = jnp.log(1.0 - 2.0 ** (-5.0 - jnp.arange(heads, dtype=F32)))
    pos = jnp.arange(RET_BLOCK)
    diff = (pos[:, None] - pos[None, :]).astype(F32)
    same_chunk = (pos[:, None] // RET_CHUNK) == (pos[None, :] // RET_CHUNK)
    earlier_chunk = (pos[None, :] // RET_CHUNK) < (pos[:, None] // RET_CHUNK)
    lg = log_gamma[:, None, None]
    dmat = jnp.where(same_chunk[None], jnp.exp(lg * jnp.abs(diff)[None]),
                     jnp.where(earlier_chunk[None], jnp.exp(lg * diff[None]), 0.0))
    idx = pos.astype(F32)
    qdec = jnp.exp(log_gamma[:, None] * (idx + 1.0))[:, :, None]
    kdec = jnp.exp(log_gamma[:, None] * (RET_BLOCK - 1.0 - idx))[:, :, None]
    cdec = jnp.exp(log_gamma * RET_BLOCK)[:, None, None]
    return jnp.cos(ang), jnp.sin(ang), dmat, qdec, kdec, cdec


def _retention(proj, batch, seq, heads):
    t = proj.shape[0]
    blk = RET_BLOCK
    nblk = seq // blk
    dk, dv = RET_QK_DIM, RET_V_DIM
    cos, sin, dmat, qdec, kdec, cdec = _retention_tables(seq, heads)
    row = lambda b, h, i: b * nblk + i
    return pl.pallas_call(
        _retention_kernel,
        grid=(batch, heads, nblk),
        in_specs=[
            pl.BlockSpec((blk, dk), lambda b, h, i: (row(b, h, i), h)),
            pl.BlockSpec((blk, dk), lambda b, h, i: (row(b, h, i), heads + h)),
            pl.BlockSpec((blk, dv), lambda b, h, i: (row(b, h, i), heads + h)),
            pl.BlockSpec((blk, dv), lambda b, h, i: (row(b, h, i), 2 * heads + h)),
            pl.BlockSpec((blk, dk // 2), lambda b, h, i: (i, 0)),
            pl.BlockSpec((blk, dk // 2), lambda b, h, i: (i, 0)),
            pl.BlockSpec((None, blk, blk), lambda b, h, i: (h, 0, 0)),
            pl.BlockSpec((None, blk, 1), lambda b, h, i: (h, 0, 0)),
            pl.BlockSpec((None, blk, 1), lambda b, h, i: (h, 0, 0)),
            pl.BlockSpec((None, 1, 1), lambda b, h, i: (h, 0, 0)),
        ],
        out_specs=pl.BlockSpec((blk, dv), lambda b, h, i: (row(b, h, i), h)),
        out_shape=jax.ShapeDtypeStruct((t, heads * dv), BF16),
        scratch_shapes=[pltpu.VMEM((dk, dv), F32)],
        compiler_params=_cparams(("parallel", "parallel", "arbitrary")),
        name="retention",
    )(proj, proj, proj, proj, cos, sin, dmat, qdec, kdec, cdec)


def _sb_kernel(q_ref, k_ref, v_ref, u_ref, o_ref, *, head_dim):
    tq = SB_TILE
    i = pl.program_id(2)
    q = q_ref[...]
    lane = lax.broadcasted_iota(jnp.int32, (tq, LANES), 1)
    first = lane < head_dim
    zero = jnp.zeros_like(q)
    q_heads = (jnp.where(first, q, zero), jnp.where(first, zero, q))
    scale = head_dim ** -0.5
    u = u_ref[...]
    row = lax.broadcasted_iota(jnp.int32, (tq, tq), 0)
    col = lax.broadcasted_iota(jnp.int32, (tq, tq), 1)
    past = col < row

    def tile(j, carry, diagonal):
        run0, run1, acc = carry
        start = pl.multiple_of(j * tq, tq)
        kt = k_ref[pl.ds(start, tq), :]
        vt = v_ref[pl.ds(start, tq), :]
        new_runs = []
        pvs = []
        for qh, run in zip(q_heads, (run0, run1)):
            z = lax.dot_general(qh, kt, (((1,), (1,)), ((), ())), preferred_element_type=F32) * scale
            sp = jnp.maximum(z, 0.0) + jnp.log1p(jnp.exp(-jnp.abs(z)))
            log_not = -sp
            if diagonal:
                log_not = jnp.where(past, log_not, 0.0)
            hi = log_not.astype(BF16)
            lo = (log_not - hi.astype(F32)).astype(BF16)
            cs = (jnp.dot(hi, u, preferred_element_type=F32) + jnp.dot(lo, u, preferred_element_type=F32))
            later = cs[:, :tq] + jnp.concatenate([run, run], axis=1)
            a = jnp.exp((z - sp) + later)
            if diagonal:
                a = jnp.where(past, a, 0.0)
            pvs.append(jnp.dot(a.astype(BF16), vt, preferred_element_type=F32))
            new_runs.append(run + cs[:, tq:])
        acc = acc + jnp.where(first, pvs[0], pvs[1])
        return new_runs[0], new_runs[1], acc

    zeros = jnp.zeros((tq, LANES), F32)
    run0, run1, acc = tile(i, (zeros, zeros, zeros), True)

    def cond(c):
        j, rmax = c[0], c[1]
        return jnp.logical_and(j >= 0, rmax >= EXP_ZERO_BELOW)

    def body(c):
        j, _, r0, r1, ac = c
        r0, r1, ac = tile(j, (r0, r1, ac), False)
        return j - 1, jnp.max(jnp.maximum(r0, r1)), r0, r1, ac

    init = (i - 1, jnp.max(jnp.maximum(run0, run1)), run0, run1, acc)
    acc = lax.while_loop(cond, body, init)[4]
    o_ref[...] = acc.astype(o_ref.dtype)


def _sb_attention(proj, batch, seq, d_model):
    t = proj.shape[0]
    head_dim = d_model // SB_HEADS
    assert 2 * head_dim == LANES
    groups = d_model // LANES
    tq = SB_TILE
    nq = seq // tq
    j = np.arange(tq)
    strictly_later = (j[:, None] > j[None, :]).astype(np.float32)
    u = jnp.asarray(np.concatenate([strictly_later, np.ones((tq, LANES), np.float32)], axis=1), BF16)
    return pl.pallas_call(
        functools.partial(_sb_kernel, head_dim=head_dim),
        grid=(batch, groups, nq),
        in_specs=[
            pl.BlockSpec((tq, LANES), lambda b, p, i: (b * nq + i, p)),
            pl.BlockSpec((seq, LANES), lambda b, p, i: (b, groups + p)),
            pl.BlockSpec((seq, LANES), lambda b, p, i: (b, 2 * groups + p)),
            pl.BlockSpec((tq, tq + LANES), lambda b, p, i: (0, 0)),
        ],
        out_specs=pl.BlockSpec((tq, LANES), lambda b, p, i: (b * nq + i, p)),
        out_shape=jax.ShapeDtypeStruct((t, d_model), BF16),
        compiler_params=_cparams(("parallel", "parallel", "arbitrary")),
        name="stick_breaking",
    )(proj, proj, proj, u)


def _layer_norm(h, g, b):
    mu = jnp.mean(h, axis=-1, keepdims=True)
    var = jnp.mean(jnp.square(h - mu), axis=-1, keepdims=True)
    return (h - mu) * lax.rsqrt(var + LN_EPS) * g + b


def _split_bf16(x):
    hi = x.astype(BF16)
    return hi, (x - hi.astype(F32)).astype(BF16)


def _mix_router_kernel(a_ref, w_ref, x_ref, g_ref, b_ref, rw_ref, rb_ref, tri_ref,
                       x1_ref, idx_ref, gate_ref, cnt_ref, carry_ref, *, alpha):
    tm = a_ref.shape[0]

    @pl.when(pl.program_id(0) == 0)
    def _():
        carry_ref[...] = jnp.zeros_like(carry_ref)

    y = jnp.dot(a_ref[...], w_ref[...], preferred_element_type=F32)
    x1 = _layer_norm(alpha * x_ref[...] + y, g_ref[...], b_ref[...])
    x1_ref[...] = x1

    xh, xl = _split_bf16(x1)
    wh, wl = _split_bf16(rw_ref[...])
    logits = (jnp.dot(xh, wh, preferred_element_type=F32) + jnp.dot(xl, wh, preferred_element_type=F32)
              + jnp.dot(xh, wl, preferred_element_type=F32)) + rb_ref[...]

    lane = lax.broadcasted_iota(jnp.int32, (tm, LANES), 1)
    work = logits
    sel_idx, sel_val, onehots = [], [], []
    for _ in range(TOP_K):
        m = jnp.max(work, axis=-1, keepdims=True)
        sel = jnp.min(jnp.where(work == m, lane, LANES), axis=-1, keepdims=True)
        hit = lane == sel
        sel_idx.append(sel)
        sel_val.append(m)
        onehots.append(hit)
        work = jnp.where(hit, -jnp.inf, work)
    exps = [jnp.exp(v - sel_val[0]) for v in sel_val]
    denom = exps[0] + exps[1] + exps[2] + exps[3]

    member = jnp.zeros((tm, LANES), F32)
    for hit in onehots:
        member = member + jnp.where(hit, 1.0, 0.0)
    prefix = jnp.dot(tri_ref[...], member.astype(BF16), preferred_element_type=F32)
    base = carry_ref[0:1, :] + prefix
    idx_out = jnp.zeros((tm, LANES), jnp.int32)
    gate_out = jnp.zeros((tm, LANES), F32)
    for k in range(TOP_K):
        rank = jnp.sum(jnp.where(onehots[k], base, 0.0), axis=-1, keepdims=True).astype(jnp.int32)
        idx_out = jnp.where(lane == k, sel_idx[k], idx_out)
        idx_out = jnp.where(lane == TOP_K + k, rank, idx_out)
        gate_out = jnp.where(lane == k, exps[k] / denom, gate_out)
    idx_ref[...] = idx_out
    gate_ref[...] = gate_out
    carry_ref[...] = carry_ref[...] + jnp.sum(member, axis=0, keepdims=True)
    cnt_ref[...] = carry_ref[...]


def _mix_router(a, w_out, x, ln_g, ln_b, router_w, router_b, alpha, tm=512):
    t, kin = a.shape
    d = x.shape[1]
    e = router_w.shape[1]
    tm = min(tm, t)
    rw = jnp.pad(router_w, ((0, 0), (0, LANES - e)))
    rb = jnp.pad(router_b, (0, LANES - e), constant_values=-jnp.inf).reshape(1, LANES)
    r = np.arange(tm)
    tri = jnp.asarray((r[None, :] < r[:, None]).astype(np.float32), BF16)
    tile = lambda i: (i, 0)
    const = lambda i: (0, 0)
    return pl.pallas_call(
        functools.partial(_mix_router_kernel, alpha=alpha),
        grid=(t // tm,),
        in_specs=[
            pl.BlockSpec((tm, kin), tile),
            pl.BlockSpec((kin, d), const),
            pl.BlockSpec((tm, d), tile),
            pl.BlockSpec((1, d), const),
            pl.BlockSpec((1, d), const),
            pl.BlockSpec((d, LANES), const),
            pl.BlockSpec((1, LANES), const),
            pl.BlockSpec((tm, tm), const),
        ],
        out_specs=[
            pl.BlockSpec((tm, d), tile),
            pl.BlockSpec((tm, LANES), tile),
            pl.BlockSpec((tm, LANES), tile),
            pl.BlockSpec((8, LANES), const),
        ],
        out_shape=[
            jax.ShapeDtypeStruct((t, d), F32),
            jax.ShapeDtypeStruct((t, LANES), jnp.int32),
            jax.ShapeDtypeStruct((t, LANES), F32),
            jax.ShapeDtypeStruct((8, LANES), F32),
        ],
        scratch_shapes=[pltpu.VMEM((8, LANES), F32)],
        compiler_params=_cparams(("arbitrary",)),
        name="mix_ln_router",
    )(a, w_out, x, ln_g.reshape(1, d), ln_b.reshape(1, d), rw, rb, tri)


def _dispatch_kernel(dest_ref, x_ref, xs_in_ref, xs_ref, sem):
    del xs_in_ref
    tm = x_ref.shape[0]
    base = pl.program_id(0) * (tm * TOP_K)

    def row_copy(t, d):
        return pltpu.make_async_copy(x_ref.at[pl.ds(t, 1), :], xs_ref.at[pl.ds(d, 1), :], sem)

    def issue(t, c):
        for k in range(TOP_K):
            row_copy(t, dest_ref[base + t * TOP_K + k]).start()
        return c

    lax.fori_loop(0, tm, issue, 0)

    def drain(t, c):
        for k in range(TOP_K):
            row_copy(t, dest_ref[base + t * TOP_K + k]).wait()
        return c

    lax.fori_loop(0, tm, drain, 0)


def _dispatch(dest, x1, n_rows, tm=512):
    t, d = x1.shape
    tm = min(tm, t)
    zeros = jnp.zeros((n_rows, d), x1.dtype)
    return pl.pallas_call(
        _dispatch_kernel,
        grid_spec=pltpu.PrefetchScalarGridSpec(
            num_scalar_prefetch=1,
            grid=(t // tm,),
            in_specs=[pl.BlockSpec((tm, d), lambda i, dest: (i, 0)),
                      pl.BlockSpec(memory_space=pl.ANY)],
            out_specs=pl.BlockSpec(memory_space=pl.ANY),
            scratch_shapes=[pltpu.SemaphoreType.DMA],
        ),
        out_shape=jax.ShapeDtypeStruct((n_rows, d), x1.dtype),
        input_output_aliases={2: 0},
        compiler_params=_cparams(("arbitrary",)),
        name="dispatch_rows",
    )(dest, x1, zeros)


def _experts_kernel(be_ref, xs_ref, wgu_ref, bg_ref, bu_ref, wd_ref, bd_ref, perm_ref,
                    ys_ref, wg_s, wu_s, wd_s):
    i = pl.program_id(0)
    e = be_ref[i]
    prev = be_ref[jnp.maximum(i - 1, 0)]
    f = wd_ref.shape[0]

    @pl.when(jnp.logical_or(i == 0, e != prev))
    def _():
        half = MXU_DIM // 2
        for c in range(2 * f // MXU_DIM):
            blk = wgu_ref[:, c * MXU_DIM:(c + 1) * MXU_DIM].astype(BF16)
            sep = jnp.dot(blk, perm_ref[...], preferred_element_type=F32).astype(BF16)
            wg_s[:, c * half:(c + 1) * half] = sep[:, :half]
            wu_s[:, c * half:(c + 1) * half] = sep[:, half:]
        wd_s[...] = wd_ref[...].astype(BF16)

    x = xs_ref[...].astype(BF16)
    gate = jnp.dot(x, wg_s[...], preferred_element_type=F32) + bg_ref[...]
    up = jnp.dot(x, wu_s[...], preferred_element_type=F32) + bu_ref[...]
    gate = jnp.minimum(gate, SWIGLU_LIMIT)
    up = jnp.clip(up, -SWIGLU_LIMIT, SWIGLU_LIMIT)
    act = (up + 1.0) * (gate * jax.nn.sigmoid(gate * SWIGLU_ALPHA))
    ys_ref[...] = jnp.dot(act.astype(BF16), wd_s[...], preferred_element_type=F32) + bd_ref[...]


def _experts(block_expert, xs, w_gate_up, b_gate, b_up, w_down, b_down):
    n_rows, d = xs.shape
    n_exp, _, f2 = w_gate_up.shape
    f = f2 // 2
    tm = ROW_BLOCK
    c = np.arange(MXU_DIM)
    src = np.where(c < MXU_DIM // 2, 2 * c, 2 * (c - MXU_DIM // 2) + 1)
    perm = jnp.asarray((np.arange(MXU_DIM)[:, None] == src[None, :]).astype(np.float32), BF16)
    by_expert = lambda i, be: (be[i], 0, 0)
    return pl.pallas_call(
        _experts_kernel,
        grid_spec=pltpu.PrefetchScalarGridSpec(
            num_scalar_prefetch=1,
            grid=(n_rows // tm,),
            in_specs=[
                pl.BlockSpec((tm, d), lambda i, be: (i, 0)),
                pl.BlockSpec((None, d, f2), by_expert),
                pl.BlockSpec((None, 1, f), by_expert),
                pl.BlockSpec((None, 1, f), by_expert),
                pl.BlockSpec((None, f, d), by_expert),
                pl.BlockSpec((None, 1, d), by_expert),
                pl.BlockSpec((MXU_DIM, MXU_DIM), lambda i, be: (0, 0)),
            ],
            out_specs=pl.BlockSpec((tm, d), lambda i, be: (i, 0)),
            scratch_shapes=[pltpu.VMEM((d, f), BF16), pltpu.VMEM((d, f), BF16), pltpu.VMEM((f, d), BF16)],
        ),
        out_shape=jax.ShapeDtypeStruct((n_rows, d), F32),
        compiler_params=_cparams(("arbitrary",)),
        name="expert_swiglu",
    )(block_expert, xs, w_gate_up, b_gate.reshape(n_exp, 1, f), b_up.reshape(n_exp, 1, f),
      w_down, b_down.reshape(n_exp, 1, d), perm)


def _combine_kernel(dest_ref, ys_ref, gate_ref, x1_ref, g_ref, b_ref, pgw_ref, pgb_ref, p_ref, pw_ref,
                    out_ref, outb_ref, buf, sem, *, alpha):
    tm = x1_ref.shape[0]
    base = pl.program_id(0) * (tm * TOP_K)

    def row_copy(t, k, d):
        return pltpu.make_async_copy(ys_ref.at[pl.ds(d, 1), :], buf.at[k, pl.ds(t, 1), :], sem)

    def issue(t, c):
        for k in range(TOP_K):
            row_copy(t, k, dest_ref[base + t * TOP_K + k]).start()
        return c

    lax.fori_loop(0, tm, issue, 0)

    def drain(t, c):
        for k in range(TOP_K):
            row_copy(t, k, dest_ref[base + t * TOP_K + k]).wait()
        return c

    lax.fori_loop(0, tm, drain, 0)

    gates = gate_ref[...]
    ffn = gates[:, 0:1] * buf[0]
    for k in range(1, TOP_K):
        ffn = ffn + gates[:, k:k + 1] * buf[k]
    x2 = _layer_norm(alpha * x1_ref[...] + ffn, g_ref[...], b_ref[...])
    ple_gate = jax.nn.sigmoid(jnp.dot(x2.astype(BF16), pgw_ref[...], preferred_element_type=F32) + pgb_ref[...])
    emb = jnp.dot(p_ref[...].astype(BF16), pw_ref[...], preferred_element_type=F32)
    out = x2 + ple_gate * emb
    out_ref[...] = out
    outb_ref[...] = out.astype(BF16)


def _combine(dest, ys, gates, x1, ln_g, ln_b, ple_gate_w, ple_gate_b, p, ple_w, alpha, tm=256):
    t, d = x1.shape
    dp = p.shape[1]
    tm = min(tm, t)
    tile = lambda i, dest: (i, 0)
    const = lambda i, dest: (0, 0)
    return pl.pallas_call(
        functools.partial(_combine_kernel, alpha=alpha),
        grid_spec=pltpu.PrefetchScalarGridSpec(
            num_scalar_prefetch=1,
            grid=(t // tm,),
            in_specs=[
                pl.BlockSpec(memory_space=pl.ANY),
                pl.BlockSpec((tm, LANES), tile),
                pl.BlockSpec((tm, d), tile),
                pl.BlockSpec((1, d), const),
                pl.BlockSpec((1, d), const),
                pl.BlockSpec((d, d), const),
                pl.BlockSpec((1, d), const),
                pl.BlockSpec((tm, dp), tile),
                pl.BlockSpec((dp, d), const),
            ],
            out_specs=[pl.BlockSpec((tm, d), tile), pl.BlockSpec((tm, d), tile)],
            scratch_shapes=[pltpu.VMEM((TOP_K, tm, d), F32), pltpu.SemaphoreType.DMA],
        ),
        out_shape=[jax.ShapeDtypeStruct((t, d), F32), jax.ShapeDtypeStruct((t, d), BF16)],
        compiler_params=_cparams(("arbitrary",)),
        name="combine_ln_ple",
    )(dest, ys, gates, x1, ln_g.reshape(1, d), ln_b.reshape(1, d), ple_gate_w, ple_gate_b.reshape(1, d), p, ple_w)


def _routing(idx_rank, counts, n_exp, n_blocks):
    top_idx = idx_rank[:, :TOP_K]
    rank = idx_rank[:, TOP_K:2 * TOP_K]
    counts = counts[0, :n_exp].astype(jnp.int32)
    padded = (counts + ROW_BLOCK - 1) // ROW_BLOCK * ROW_BLOCK
    pad_ends = jnp.cumsum(padded)
    pad_starts = pad_ends - padded
    dest = (pad_starts[top_idx] + rank).reshape(-1).astype(jnp.int32)
    block_start = jnp.arange(n_blocks, dtype=jnp.int32) * ROW_BLOCK
    last_used = jnp.maximum(pad_ends[-1] - 1, 0)
    block_expert = jnp.searchsorted(pad_ends, jnp.minimum(block_start, last_used), side='right')
    return dest, jnp.minimum(block_expert, n_exp - 1).astype(jnp.int32)


def kernel(x, p, ret_w_in, ret_w_out, sb_w_in, sb_w_out, ln1_g, ln1_b, router_w, router_b, w_gate_up, b_gate_up,
           w_down, b_down, ln2_g, ln2_b, ple_w, ple_gate_w, ple_gate_b):
    batch, seq, d = x.shape
    depth = ln1_g.shape[0]
    n_exp = router_w.shape[-1]
    t = batch * seq
    alpha = float((2 * depth) ** 0.25)
    ret_heads = d // RET_QK_DIM
    n_blocks = -(-(t * TOP_K + n_exp * (ROW_BLOCK - 1)) // ROW_BLOCK)
    n_rows = n_blocks * ROW_BLOCK

    xf = x.reshape(t, d)
    xb = xf.astype(BF16)
    for i in range(depth):
        j = i // 2
        if i % 2 == 0:
            proj = _matmul(xb, ret_w_in[j].astype(BF16), BF16)
            mixed = _retention(proj, batch, seq, ret_heads)
            w_out = ret_w_out[j].astype(BF16)
        else:
            proj = _matmul(xb, sb_w_in[j].astype(BF16), BF16)
            mixed = _sb_attention(proj, batch, seq, d)
            w_out = sb_w_out[j].astype(BF16)
        x1, idx_rank, gates, counts = _mix_router(mixed, w_out, xf, ln1_g[i], ln1_b[i], router_w[i], router_b[i], alpha)
        dest, block_expert = _routing(idx_rank, counts, n_exp, n_blocks)
        xs = _dispatch(dest, x1, n_rows)
        ys = _experts(block_expert, xs, w_gate_up[i], b_gate_up[i][:, 0::2], b_gate_up[i][:, 1::2],
                      w_down[i], b_down[i])
        xf, xb = _combine(dest, ys, gates, x1, ln2_g[i], ln2_b[i], ple_gate_w[i].astype(BF16), ple_gate_b[i],
                          p[i].reshape(t, -1), ple_w[i].astype(BF16), alpha)
    return xf.reshape(batch, seq, d)
```

```python
import functools

import jax
import jax.numpy as jnp
import numpy as np
from jax import lax
from jax.experimental import pallas as pl
from jax.experimental.pallas import tpu as pltpu

F32 = jnp.float32
BF16 = jnp.bfloat16

RET_QK_DIM = 256
RET_V_DIM = 512
ROPE_BASE = 10000.0
GN_EPS = 1e-6
LN_EPS = 1e-5
SB_HEADS = 16
TOP_K = 4
SWIGLU_LIMIT = 7.0
SWIGLU_ALPHA = 1.702

LANES = 128
MXU_DIM = 256
VMEM_LIMIT = 56 * 1024 * 1024

RET_BLOCK = 256
RET_CHUNK = 64
SB_TILE = 256
ROW_BLOCK = 256
EXP_ZERO_BELOW = -104.0


def _cparams(sem):
    return pltpu.CompilerParams(dimension_semantics=sem, vmem_limit_bytes=VMEM_LIMIT)


def _mm_kernel(x_ref, w_ref, o_ref):
    o_ref[...] = jnp.dot(x_ref[...], w_ref[...], preferred_element_type=F32).astype(o_ref.dtype)


def _matmul(x, w, out_dtype, tm=1024, tn=1024):
    m, k = x.shape
    n = w.shape[1]
    tm = min(tm, m)
    tn = min(tn, n)
    return pl.pallas_call(
        _mm_kernel,
        grid=(n // tn, m // tm),
        in_specs=[pl.BlockSpec((tm, k), lambda j, i: (i, 0)),
                  pl.BlockSpec((k, tn), lambda j, i: (0, j))],
        out_specs=pl.BlockSpec((tm, tn), lambda j, i: (i, j)),
        out_shape=jax.ShapeDtypeStruct((m, n), out_dtype),
        compiler_params=_cparams(("parallel", "parallel")),
        name="proj_matmul",
    )(x, w)


def _retention_kernel(q_ref, k_ref, v_ref, g_ref, cos_ref, sin_ref, dmat_ref, qdec_ref, kdec_ref,
                      cdec_ref, o_ref, state_ref):
    half = RET_QK_DIM // 2
    cos = cos_ref[...]
    sin = sin_ref[...]

    def rot(t):
        t1, t2 = t[:, :half], t[:, half:]
        return jnp.concatenate([t1 * cos - t2 * sin, t1 * sin + t2 * cos], axis=-1)

    q = rot(q_ref[...].astype(F32))
    k = rot(k_ref[...].astype(F32)) * (RET_QK_DIM ** -0.5)
    v = v_ref[...]

    @pl.when(pl.program_id(2) == 0)
    def _():
        state_ref[...] = jnp.zeros_like(state_ref)

    state = state_ref[...]
    scores = lax.dot_general(q.astype(BF16), k.astype(BF16), (((1,), (1,)), ((), ())),
                             preferred_element_type=F32) * dmat_ref[...]
    inner = jnp.dot(scores.astype(BF16), v, preferred_element_type=F32)
    cross = jnp.dot((q * qdec_ref[...]).astype(BF16), state.astype(BF16), preferred_element_type=F32)
    kv = lax.dot_general((k * kdec_ref[...]).astype(BF16), v, (((0,), (0,)), ((), ())),
                         preferred_element_type=F32)
    state_ref[...] = state * cdec_ref[...] + kv

    out = inner + cross
    mu = jnp.mean(out, axis=-1, keepdims=True)
    var = jnp.mean(jnp.square(out - mu), axis=-1, keepdims=True)
    normed = (out - mu) * lax.rsqrt(var + GN_EPS)
    g = g_ref[...].astype(F32)
    o_ref[...] = (g * jax.nn.sigmoid(g) * normed).astype(o_ref.dtype)


def _retention_tables(seq, heads):
    half = RET_QK_DIM // 2
    inv_freq = 1.0 / (ROPE_BASE ** (jnp.arange(half, dtype=F32) / half))
    ang = jnp.arange(seq, dtype=F32)[:, None] * inv_freq[None, :]
    log_gamma = jnp.log(1.0 - 2.0 ** (-5.0 - jnp.arange(heads, dtype=F32)))
    pos = jnp.arange(RET_BLOCK)
    diff = (pos[:, None] - pos[None, :]).astype(F32)
    same_chunk = (pos[:, None] // RET_CHUNK) == (pos[None, :] // RET_CHUNK)
    earlier_chunk = (pos[None, :] // RET_CHUNK) < (pos[:, None] // RET_CHUNK)
    lg = log_gamma[:, None, None]
    dmat = jnp.where(same_chunk[None], jnp.exp(lg * jnp.abs(diff)[None]),
                     jnp.where(earlier_chunk[None], jnp.exp(lg * diff[None]), 0.0))
    idx = pos.astype(F32)
    qdec = jnp.exp(log_gamma[:, None] * (idx + 1.0))[:, :, None]
    kdec = jnp.exp(log_gamma[:, None] * (RET_BLOCK - 1.0 - idx))[:, :, None]
    cdec = jnp.exp(log_gamma * RET_BLOCK)[:, None, None]
    return jnp.cos(ang), jnp.sin(ang), dmat, qdec, kdec, cdec


def _retention(proj, batch, seq, heads):
    t = proj.shape[0]
    blk = RET_BLOCK
    nblk = seq // blk
    dk, dv = RET_QK_DIM, RET_V_DIM
    cos, sin, dmat, qdec, kdec, cdec = _retention_tables(seq, heads)
    row = lambda b, h, i: b * nblk + i
    return pl.pallas_call(
        _retention_kernel,
        grid=(batch, heads, nblk),
        in_specs=[
            pl.BlockSpec((blk, dk), lambda b, h, i: (row(b, h, i), h)),
            pl.BlockSpec((blk, dk), lambda b, h, i: (row(b, h, i), heads + h)),
            pl.BlockSpec((blk, dv), lambda b, h, i: (row(b, h, i), heads + h)),
            pl.BlockSpec((blk, dv), lambda b, h, i: (row(b, h, i), 2 * heads + h)),
            pl.BlockSpec((blk, dk // 2), lambda b, h, i: (i, 0)),
            pl.BlockSpec((blk, dk // 2), lambda b, h, i: (i, 0)),
            pl.BlockSpec((None, blk, blk), lambda b, h, i: (h, 0, 0)),
            pl.BlockSpec((None, blk, 1), lambda b, h, i: (h, 0, 0)),
            pl.BlockSpec((None, blk, 1), lambda b, h, i: (h, 0, 0)),
            pl.BlockSpec((None, 1, 1), lambda b, h, i: (h, 0, 0)),
        ],
        out_specs=pl.BlockSpec((blk, dv), lambda b, h, i: (row(b, h, i), h)),
        out_shape=jax.ShapeDtypeStruct((t, heads * dv), BF16),
        scratch_shapes=[pltpu.VMEM((dk, dv), F32)],
        compiler_params=_cparams(("parallel", "parallel", "arbitrary")),
        name="retention",
    )(proj, proj, proj, proj, cos, sin, dmat, qdec, kdec, cdec)


def _sb_kernel(q_ref, k_ref, v_ref, u_ref, o_ref, *, head_dim):
    tq = SB_TILE
    i = pl.program_id(2)
    q = q_ref[...]
    lane = lax.broadcasted_iota(jnp.int32, (tq, LANES), 1)
    first = lane < head_dim
    zero = jnp.zeros_like(q)
    q_heads = (jnp.where(first, q, zero), jnp.where(first, zero, q))
    scale = head_dim ** -0.5
    u = u_ref[...]
    row = lax.broadcasted_iota(jnp.int32, (tq, tq), 0)
    col = lax.broadcasted_iota(jnp.int32, (tq, tq), 1)
    past = col < row

    def tile(j, carry, diagonal):
        run0, run1, acc = carry
        start = pl.multiple_of(j * tq, tq)
        kt = k_ref[pl.ds(start, tq), :]
        vt = v_ref[pl.ds(start, tq), :]
        new_runs = []
        pvs = []
        for qh, run in zip(q_heads, (run0, run1)):
            z = lax.dot_general(qh, kt, (((1,), (1,)), ((), ())), preferred_element_type=F32) * scale
            sp = jnp.maximum(z, 0.0) + jnp.log1p(jnp.exp(-jnp.abs(z)))
            log_not = -sp
            if diagonal:
                log_not = jnp.where(past, log_not, 0.0)
            hi = log_not.astype(BF16)
            lo = (log_not - hi.astype(F32)).astype(BF16)
            cs = (jnp.dot(hi, u, preferred_element_type=F32) + jnp.dot(lo, u, preferred_element_type=F32))
            later = cs[:, :tq] + jnp.concatenate([run, run], axis=1)
            a = jnp.exp((z - sp) + later)
            if diagonal:
                a = jnp.where(past, a, 0.0)
            pvs.append(jnp.dot(a.astype(BF16), vt, preferred_element_type=F32))
            new_runs.append(run + cs[:, tq:])
        acc = acc + jnp.where(first, pvs[0], pvs[1])
        return new_runs[0], new_runs[1], acc

    zeros = jnp.zeros((tq, LANES), F32)
    run0, run1, acc = tile(i, (zeros, zeros, zeros), True)

    def cond(c):
        j, rmax = c[0], c[1]
        return jnp.logical_and(j >= 0, rmax >= EXP_ZERO_BELOW)

    def body(c):
        j, _, r0, r1, ac = c
        r0, r1, ac = tile(j, (r0, r1, ac), False)
        return j - 1, jnp.max(jnp.maximum(r0, r1)), r0, r1, ac

    init = (i - 1, jnp.max(jnp.maximum(run0, run1)), run0, run1, acc)
    acc = lax.while_loop(cond, body, init)[4]
    o_ref[...] = acc.astype(o_ref.dtype)


def _sb_attention(proj, batch, seq, d_model):
    t = proj.shape[0]
    head_dim = d_model // SB_HEADS
    assert 2 * head_dim == LANES
    groups = d_model // LANES
    tq = SB_TILE
    nq = seq // tq
    j = np.arange(tq)
    strictly_later = (j[:, None] > j[None, :]).astype(np.float32)
    u = jnp.asarray(np.concatenate([strictly_later, np.ones((tq, LANES), np.float32)], axis=1), BF16)
    return pl.pallas_call(
        functools.partial(_sb_kernel, head_dim=head_dim),
        grid=(batch, groups, nq),
        in_specs=[
            pl.BlockSpec((tq, LANES), lambda b, p, i: (b * nq + i, p)),
            pl.BlockSpec((seq, LANES), lambda b, p, i: (b, groups + p)),
            pl.BlockSpec((seq, LANES), lambda b, p, i: (b, 2 * groups + p)),
            pl.BlockSpec((tq, tq + LANES), lambda b, p, i: (0, 0)),
        ],
        out_specs=pl.BlockSpec((tq, LANES), lambda b, p, i: (b * nq + i, p)),
        out_shape=jax.ShapeDtypeStruct((t, d_model), BF16),
        compiler_params=_cparams(("parallel", "parallel", "arbitrary")),
        name="stick_breaking",
    )(proj, proj, proj, u)


def _layer_norm(h, g, b):
    mu = jnp.mean(h, axis=-1, keepdims=True)
    var = jnp.mean(jnp.square(h - mu), axis=-1, keepdims=True)
    return (h - mu) * lax.rsqrt(var + LN_EPS) * g + b


def _split_bf16(x):
    hi = x.astype(BF16)
    return hi, (x - hi.astype(F32)).astype(BF16)


def _mix_router_kernel(a_ref, w_ref, x_ref, g_ref, b_ref, rw_ref, rb_ref, tri_ref,
                       x1_ref, idx_ref, gate_ref, cnt_ref, carry_ref, *, alpha):
    tm = a_ref.shape[0]

    @pl.when(pl.program_id(0) == 0)
    def _():
        carry_ref[...] = jnp.zeros_like(carry_ref)

    y = jnp.dot(a_ref[...], w_ref[...], preferred_element_type=F32)
    x1 = _layer_norm(alpha * x_ref[...] + y, g_ref[...], b_ref[...])
    x1_ref[...] = x1

    xh, xl = _split_bf16(x1)
    wh, wl = _split_bf16(rw_ref[...])
    logits = (jnp.dot(xh, wh, preferred_element_type=F32) + jnp.dot(xl, wh, preferred_element_type=F32)
              + jnp.dot(xh, wl, preferred_element_type=F32)) + rb_ref[...]

    lane = lax.broadcasted_iota(jnp.int32, (tm, LANES), 1)
    work = logits
    sel_idx, sel_val, onehots = [], [], []
    for _ in range(TOP_K):
        m = jnp.max(work, axis=-1, keepdims=True)
        sel = jnp.min(jnp.where(work == m, lane, LANES), axis=-1, keepdims=True)
        hit = lane == sel
        sel_idx.append(sel)
        sel_val.append(m)
        onehots.append(hit)
        work = jnp.where(hit, -jnp.inf, work)
    exps = [jnp.exp(v - sel_val[0]) for v in sel_val]
    denom = exps[0] + exps[1] + exps[2] + exps[3]

    member = jnp.zeros((tm, LANES), F32)
    for hit in onehots:
        member = member + jnp.where(hit, 1.0, 0.0)
    prefix = jnp.dot(tri_ref[...], member.astype(BF16), preferred_element_type=F32)
    base = carry_ref[0:1, :] + prefix
    idx_out = jnp.zeros((tm, LANES), jnp.int32)
    gate_out = jnp.zeros((tm, LANES), F32)
    for k in range(TOP_K):
        rank = jnp.sum(jnp.where(onehots[k], base, 0.0), axis=-1, keepdims=True).astype(jnp.int32)
        idx_out = jnp.where(lane == k, sel_idx[k], idx_out)
        idx_out = jnp.where(lane == TOP_K + k, rank, idx_out)
        gate_out = jnp.where(lane == k, exps[k] / denom, gate_out)
    idx_ref[...] = idx_out
    gate_ref[...] = gate_out
    carry_ref[...] = carry_ref[...] + jnp.sum(member, axis=0, keepdims=True)
    cnt_ref[...] = carry_ref[...]


def _mix_router(a, w_out, x, ln_g, ln_b, router_w, router_b, alpha, tm=512):
    t, kin = a.shape
    d = x.shape[1]
    e = router_w.shape[1]
    tm = min(tm, t)
    rw = jnp.pad(router_w, ((0, 0), (0, LANES - e)))
    rb = jnp.pad(router_b, (0, LANES - e), constant_values=-jnp.inf).reshape(1, LANES)
    r = np.arange(tm)
    tri = jnp.asarray((r[None, :] < r[:, None]).astype(np.float32), BF16)
    tile = lambda i: (i, 0)
    const = lambda i: (0, 0)
    return pl.pallas_call(
        functools.partial(_mix_router_kernel, alpha=alpha),
        grid=(t // tm,),
        in_specs=[
            pl.BlockSpec((tm, kin), tile),
            pl.BlockSpec((kin, d), const),
            pl.BlockSpec((tm, d), tile),
            pl.BlockSpec((1, d), const),
            pl.BlockSpec((1, d), const),
            pl.BlockSpec((d, LANES), const),
            pl.BlockSpec((1, LANES), const),
            pl.BlockSpec((tm, tm), const),
        ],
        out_specs=[
            pl.BlockSpec((tm, d), tile),
            pl.BlockSpec((tm, LANES), tile),
            pl.BlockSpec((tm, LANES), tile),
            pl.BlockSpec((8, LANES), const),
        ],
        out_shape=[
            jax.ShapeDtypeStruct((t, d), F32),
            jax.ShapeDtypeStruct((t, LANES), jnp.int32),
            jax.ShapeDtypeStruct((t, LANES), F32),
            jax.ShapeDtypeStruct((8, LANES), F32),
        ],
        scratch_shapes=[pltpu.VMEM((8, LANES), F32)],
        compiler_params=_cparams(("arbitrary",)),
        name="mix_ln_router",
    )(a, w_out, x, ln_g.reshape(1, d), ln_b.reshape(1, d), rw, rb, tri)


def _dispatch_kernel(dest_ref, x_ref, xs_in_ref, xs_ref, sem):
    del xs_in_ref
    tm = x_ref.shape[0]
    base = pl.program_id(0) * (tm * TOP_K)

    def row_copy(t, d):
        return pltpu.make_async_copy(x_ref.at[pl.ds(t, 1), :], xs_ref.at[pl.ds(d, 1), :], sem)

    def issue(t, c):
        for k in range(TOP_K):
            row_copy(t, dest_ref[base + t * TOP_K + k]).start()
        return c

    lax.fori_loop(0, tm, issue, 0)

    def drain(t, c):
        for k in range(TOP_K):
            row_copy(t, dest_ref[base + t * TOP_K + k]).wait()
        return c

    lax.fori_loop(0, tm, drain, 0)


def _dispatch(dest, x1, n_rows, tm=512):
    t, d = x1.shape
    tm = min(tm, t)
    zeros = jnp.zeros((n_rows, d), x1.dtype)
    return pl.pallas_call(
        _dispatch_kernel,
        grid_spec=pltpu.PrefetchScalarGridSpec(
            num_scalar_prefetch=1,
            grid=(t // tm,),
            in_specs=[pl.BlockSpec((tm, d), lambda i, dest: (i, 0)),
                      pl.BlockSpec(memory_space=pl.ANY)],
            out_specs=pl.BlockSpec(memory_space=pl.ANY),
            scratch_shapes=[pltpu.SemaphoreType.DMA],
        ),
        out_shape=jax.ShapeDtypeStruct((n_rows, d), x1.dtype),
        input_output_aliases={2: 0},
        compiler_params=_cparams(("arbitrary",)),
        name="dispatch_rows",
    )(dest, x1, zeros)


def _experts_kernel(be_ref, xs_ref, wgu_ref, bg_ref, bu_ref, wd_ref, bd_ref, perm_ref,
                    ys_ref, wg_s, wu_s, wd_s):
    i = pl.program_id(0)
    e = be_ref[i]
    prev = be_ref[jnp.maximum(i - 1, 0)]
    f = wd_ref.shape[0]

    @pl.when(jnp.logical_or(i == 0, e != prev))
    def _():
        half = MXU_DIM // 2
        for c in range(2 * f // MXU_DIM):
            blk = wgu_ref[:, c * MXU_DIM:(c + 1) * MXU_DIM].astype(BF16)
            sep = jnp.dot(blk, perm_ref[...], preferred_element_type=F32).astype(BF16)
            wg_s[:, c * half:(c + 1) * half] = sep[:, :half]
            wu_s[:, c * half:(c + 1) * half] = sep[:, half:]
        wd_s[...] = wd_ref[...].astype(BF16)

    x = xs_ref[...].astype(BF16)
    gate = jnp.dot(x, wg_s[...], preferred_element_type=F32) + bg_ref[...]
    up = jnp.dot(x, wu_s[...], preferred_element_type=F32) + bu_ref[...]
    gate = jnp.minimum(gate, SWIGLU_LIMIT)
    up = jnp.clip(up, -SWIGLU_LIMIT, SWIGLU_LIMIT)
    act = (up + 1.0) * (gate * jax.nn.sigmoid(gate * SWIGLU_ALPHA))
    ys_ref[...] = jnp.dot(act.astype(BF16), wd_s[...], preferred_element_type=F32) + bd_ref[...]


def _experts(block_expert, xs, layer, w_gate_up, b_gate, b_up, w_down, b_down):
    n_rows, d = xs.shape
    _, n_exp, _, f2 = w_gate_up.shape
    f = f2 // 2
    tm = ROW_BLOCK
    c = np.arange(MXU_DIM)
    src = np.where(c < MXU_DIM // 2, 2 * c, 2 * (c - MXU_DIM // 2) + 1)
    perm = jnp.asarray((np.arange(MXU_DIM)[:, None] == src[None, :]).astype(np.float32), BF16)
    by_expert = lambda i, be: (be[i], 0, 0)
    by_layer_expert = lambda i, be: (layer, be[i], 0, 0)
    return pl.pallas_call(
        _experts_kernel,
        grid_spec=pltpu.PrefetchScalarGridSpec(
            num_scalar_prefetch=1,
            grid=(n_rows // tm,),
            in_specs=[
                pl.BlockSpec((tm, d), lambda i, be: (i, 0)),
                pl.BlockSpec((None, None, d, f2), by_layer_expert),
                pl.BlockSpec((None, 1, f), by_expert),
                pl.BlockSpec((None, 1, f), by_expert),
                pl.BlockSpec((None, None, f, d), by_layer_expert),
                pl.BlockSpec((None, 1, d), by_expert),
                pl.BlockSpec((MXU_DIM, MXU_DIM), lambda i, be: (0, 0)),
            ],
            out_specs=pl.BlockSpec((tm, d), lambda i, be: (i, 0)),
            scratch_shapes=[pltpu.VMEM((d, f), BF16), pltpu.VMEM((d, f), BF16), pltpu.VMEM((f, d), BF16)],
        ),
        out_shape=jax.ShapeDtypeStruct((n_rows, d), F32),
        compiler_params=_cparams(("arbitrary",)),
        name="expert_swiglu",
    )(block_expert, xs, w_gate_up, b_gate.reshape(n_exp, 1, f), b_up.reshape(n_exp, 1, f),
      w_down, b_down.reshape(n_exp, 1, d), perm)


def _combine_kernel(dest_ref, ys_ref, gate_ref, x1_ref, g_ref, b_ref, pgw_ref, pgb_ref, p_ref, pw_ref,
                    out_ref, outb_ref, buf, sem, *, alpha):
    tm = x1_ref.shape[0]
    base = pl.program_id(0) * (tm * TOP_K)

    def row_copy(t, k, d):
        return pltpu.make_async_copy(ys_ref.at[pl.ds(d, 1), :], buf.at[k, pl.ds(t, 1), :], sem)

    def issue(t, c):
        for k in range(TOP_K):
            row_copy(t, k, dest_ref[base + t * TOP_K + k]).start()
        return c

    lax.fori_loop(0, tm, issue, 0)

    def drain(t, c):
        for k in range(TOP_K):
            row_copy(t, k, dest_ref[base + t * TOP_K + k]).wait()
        return c

    lax.fori_loop(0, tm, drain, 0)

    gates = gate_ref[...]
    ffn = gates[:, 0:1] * buf[0]
    for k in range(1, TOP_K):
        ffn = ffn + gates[:, k:k + 1] * buf[k]
    x2 = _layer_norm(alpha * x1_ref[...] + ffn, g_ref[...], b_ref[...])
    ple_gate = jax.nn.sigmoid(jnp.dot(x2.astype(BF16), pgw_ref[...], preferred_element_type=F32) + pgb_ref[...])
    emb = jnp.dot(p_ref[...].astype(BF16), pw_ref[...], preferred_element_type=F32)
    out = x2 + ple_gate * emb
    out_ref[...] = out
    outb_ref[...] = out.astype(BF16)


def _combine(dest, ys, gates, x1, ln_g, ln_b, ple_gate_w, ple_gate_b, p, ple_w, alpha, tm=256):
    t, d = x1.shape
    dp = p.shape[1]
    tm = min(tm, t)
    tile = lambda i, dest: (i, 0)
    const = lambda i, dest: (0, 0)
    return pl.pallas_call(
        functools.partial(_combine_kernel, alpha=alpha),
        grid_spec=pltpu.PrefetchScalarGridSpec(
            num_scalar_prefetch=1,
            grid=(t // tm,),
            in_specs=[
                pl.BlockSpec(memory_space=pl.ANY),
                pl.BlockSpec((tm, LANES), tile),
                pl.BlockSpec((tm, d), tile),
                pl.BlockSpec((1, d), const),
                pl.BlockSpec((1, d), const),
                pl.BlockSpec((d, d), const),
                pl.BlockSpec((1, d), const),
                pl.BlockSpec((tm, dp), tile),
                pl.BlockSpec((dp, d), const),
            ],
            out_specs=[pl.BlockSpec((tm, d), tile), pl.BlockSpec((tm, d), tile)],
            scratch_shapes=[pltpu.VMEM((TOP_K, tm, d), F32), pltpu.SemaphoreType.DMA],
        ),
        out_shape=[jax.ShapeDtypeStruct((t, d), F32), jax.ShapeDtypeStruct((t, d), BF16)],
        compiler_params=_cparams(("arbitrary",)),
        name="combine_ln_ple",
    )(dest, ys, gates, x1, ln_g.reshape(1, d), ln_b.reshape(1, d), ple_gate_w, ple_gate_b.reshape(1, d), p, ple_w)


def _routing(idx_rank, counts, n_exp, n_blocks):
    top_idx = idx_rank[:, :TOP_K]
    rank = idx_rank[:, TOP_K:2 * TOP_K]
    counts = counts[0, :n_exp].astype(jnp.int32)
    padded = (counts + ROW_BLOCK - 1) // ROW_BLOCK * ROW_BLOCK
    pad_ends = jnp.cumsum(padded)
    pad_starts = pad_ends - padded
    experts = jnp.arange(n_exp, dtype=jnp.int32)
    start_of = jnp.sum(jnp.where(top_idx[:, :, None] == experts, pad_starts, 0), axis=-1)
    dest = (start_of + rank).reshape(-1).astype(jnp.int32)
    block_start = jnp.arange(n_blocks, dtype=jnp.int32) * ROW_BLOCK
    last_used = jnp.maximum(pad_ends[-1] - 1, 0)
    block_expert = jnp.sum(pad_ends[None, :] <= jnp.minimum(block_start, last_used)[:, None], axis=-1)
    return dest, jnp.minimum(block_expert, n_exp - 1).astype(jnp.int32)


def kernel(x, p, ret_w_in, ret_w_out, sb_w_in, sb_w_out, ln1_g, ln1_b, router_w, router_b, w_gate_up, b_gate_up,
           w_down, b_down, ln2_g, ln2_b, ple_w, ple_gate_w, ple_gate_b):
    batch, seq, d = x.shape
    depth = ln1_g.shape[0]
    n_exp = router_w.shape[-1]
    t = batch * seq
    alpha = float((2 * depth) ** 0.25)
    ret_heads = d // RET_QK_DIM
    n_blocks = -(-(t * TOP_K + n_exp * (ROW_BLOCK - 1)) // ROW_BLOCK)
    n_rows = n_blocks * ROW_BLOCK

    xf = x.reshape(t, d)
    xb = xf.astype(BF16)
    for i in range(depth):
        j = i // 2
        if i % 2 == 0:
            proj = _matmul(xb, ret_w_in[j].astype(BF16), BF16)
            mixed = _retention(proj, batch, seq, ret_heads)
            w_out = ret_w_out[j].astype(BF16)
        else:
            proj = _matmul(xb, sb_w_in[j].astype(BF16), BF16)
            mixed = _sb_attention(proj, batch, seq, d)
            w_out = sb_w_out[j].astype(BF16)
        x1, idx_rank, gates, counts = _mix_router(mixed, w_out, xf, ln1_g[i], ln1_b[i], router_w[i], router_b[i], alpha)
        dest, block_expert = _routing(idx_rank, counts, n_exp, n_blocks)
        xs = _dispatch(dest, x1, n_rows)
        ys = _experts(block_expert, xs, i, w_gate_up, b_gate_up[i][:, 0::2], b_gate_up[i][:, 1::2],
                      w_down, b_down[i])
        xf, xb = _combine(dest, ys, gates, x1, ln2_g[i], ln2_b[i], ple_gate_w[i].astype(BF16), ple_gate_b[i],
                          p[i].reshape(t, -1), ple_w[i].astype(BF16), alpha)
    return xf.reshape(batch, seq, d)
```

```python
import functools

import jax
import jax.numpy as jnp
import numpy as np
from jax import lax
from jax.experimental import pallas as pl
from jax.experimental.pallas import tpu as pltpu

F32 = jnp.float32
BF16 = jnp.bfloat16

RET_QK_DIM = 256
RET_V_DIM = 512
ROPE_BASE = 10000.0
GN_EPS = 1e-6
LN_EPS = 1e-5
SB_HEADS = 16
TOP_K = 4
SWIGLU_LIMIT = 7.0
SWIGLU_ALPHA = 1.702

LANES = 128
MXU_DIM = 256
VMEM_LIMIT = 56 * 1024 * 1024

RET_BLOCK = 256
RET_CHUNK = 64
SB_TILE = 256
ROW_BLOCK = 256
EXP_ZERO_BELOW = -104.0


def _cparams(sem):
    return pltpu.CompilerParams(dimension_semantics=sem, vmem_limit_bytes=VMEM_LIMIT)


def _mm_kernel(x_ref, w_ref, o_ref):
    o_ref[...] = jnp.dot(x_ref[...], w_ref[...], preferred_element_type=F32).astype(o_ref.dtype)


def _matmul(x, w, out_dtype, tm=1024, tn=1024):
    m, k = x.shape
    n = w.shape[1]
    tm = min(tm, m)
    tn = min(tn, n)
    return pl.pallas_call(
        _mm_kernel,
        grid=(n // tn, m // tm),
        in_specs=[pl.BlockSpec((tm, k), lambda j, i: (i, 0)),
                  pl.BlockSpec((k, tn), lambda j, i: (0, j))],
        out_specs=pl.BlockSpec((tm, tn), lambda j, i: (i, j)),
        out_shape=jax.ShapeDtypeStruct((m, n), out_dtype),
        compiler_params=_cparams(("parallel", "parallel")),
        name="proj_matmul",
    )(x, w)


def _retention_kernel(q_ref, k_ref, v_ref, g_ref, cos_ref, sin_ref, dmat_ref, qdec_ref, kdec_ref,
                      cdec_ref, o_ref, state_ref):
    half = RET_QK_DIM // 2
    cos = cos_ref[...]
    sin = sin_ref[...]

    def rot(t):
        t1, t2 = t[:, :half], t[:, half:]
        return jnp.concatenate([t1 * cos - t2 * sin, t1 * sin + t2 * cos], axis=-1)

    q = rot(q_ref[...].astype(F32))
    k = rot(k_ref[...].astype(F32)) * (RET_QK_DIM ** -0.5)
    v = v_ref[...]

    @pl.when(pl.program_id(2) == 0)
    def _():
        state_ref[...] = jnp.zeros_like(state_ref)

    state = state_ref[...]
    scores = lax.dot_general(q.astype(BF16), k.astype(BF16), (((1,), (1,)), ((), ())),
                             preferred_element_type=F32) * dmat_ref[...]
    inner = jnp.dot(scores.astype(BF16), v, preferred_element_type=F32)
    cross = jnp.dot((q * qdec_ref[...]).astype(BF16), state.astype(BF16), preferred_element_type=F32)
    kv = lax.dot_general((k * kdec_ref[...]).astype(BF16), v, (((0,), (0,)), ((), ())),
                         preferred_element_type=F32)
    state_ref[...] = state * cdec_ref[...] + kv

    out = inner + cross
    mu = jnp.mean(out, axis=-1, keepdims=True)
    var = jnp.mean(jnp.square(out - mu), axis=-1, keepdims=True)
    normed = (out - mu) * lax.rsqrt(var + GN_EPS)
    g = g_ref[...].astype(F32)
    o_ref[...] = (g * jax.nn.sigmoid(g) * normed).astype(o_ref.dtype)


def _retention_tables(seq, heads):
    half = RET_QK_DIM // 2
    inv_freq = 1.0 / (ROPE_BASE ** (jnp.arange(half, dtype=F32) / half))
    ang = jnp.arange(seq, dtype=F32)[:, None] * inv_freq[None, :]
    log_gamma = jnp.log(1.0 - 2.0 ** (-5.0 - jnp.arange(heads, dtype=F32)))
    pos = jnp.arange(RET_BLOCK)
    diff = (pos[:, None] - pos[None, :]).astype(F32)
    same_chunk = (pos[:, None] // RET_CHUNK) == (pos[None, :] // RET_CHUNK)
    earlier_chunk = (pos[None, :] // RET_CHUNK) < (pos[:, None] // RET_CHUNK)
    lg = log_gamma[:, None, None]
    dmat = jnp.where(same_chunk[None], jnp.exp(lg * jnp.abs(diff)[None]),
                     jnp.where(earlier_chunk[None], jnp.exp(lg * diff[None]), 0.0))
    idx = pos.astype(F32)
    qdec = jnp.exp(log_gamma[:, None] * (idx + 1.0))[:, :, None]
    kdec = jnp.exp(log_gamma[:, None] * (RET_BLOCK - 1.0 - idx))[:, :, None]
    cdec = jnp.exp(log_gamma * RET_BLOCK)[:, None, None]
    return jnp.cos(ang), jnp.sin(ang), dmat, qdec, kdec, cdec


def _retention(proj, batch, seq, heads):
    t = proj.shape[0]
    blk = RET_BLOCK
    nblk = seq // blk
    dk, dv = RET_QK_DIM, RET_V_DIM
    cos, sin, dmat, qdec, kdec, cdec = _retention_tables(seq, heads)
    row = lambda b, h, i: b * nblk + i
    return pl.pallas_call(
        _retention_kernel,
        grid=(batch, heads, nblk),
        in_specs=[
            pl.BlockSpec((blk, dk), lambda b, h, i: (row(b, h, i), h)),
            pl.BlockSpec((blk, dk), lambda b, h, i: (row(b, h, i), heads + h)),
            pl.BlockSpec((blk, dv), lambda b, h, i: (row(b, h, i), heads + h)),
            pl.BlockSpec((blk, dv), lambda b, h, i: (row(b, h, i), 2 * heads + h)),
            pl.BlockSpec((blk, dk // 2), lambda b, h, i: (i, 0)),
            pl.BlockSpec((blk, dk // 2), lambda b, h, i: (i, 0)),
            pl.BlockSpec((None, blk, blk), lambda b, h, i: (h, 0, 0)),
            pl.BlockSpec((None, blk, 1), lambda b, h, i: (h, 0, 0)),
            pl.BlockSpec((None, blk, 1), lambda b, h, i: (h, 0, 0)),
            pl.BlockSpec((None, 1, 1), lambda b, h, i: (h, 0, 0)),
        ],
        out_specs=pl.BlockSpec((blk, dv), lambda b, h, i: (row(b, h, i), h)),
        out_shape=jax.ShapeDtypeStruct((t, heads * dv), BF16),
        scratch_shapes=[pltpu.VMEM((dk, dv), F32)],
        compiler_params=_cparams(("parallel", "parallel", "arbitrary")),
        name="retention",
    )(proj, proj, proj, proj, cos, sin, dmat, qdec, kdec, cdec)


def _sb_kernel(q_ref, k_ref, v_ref, u_ref, o_ref, *, head_dim):
    tq = SB_TILE
    i = pl.program_id(2)
    q = q_ref[...]
    lane = lax.broadcasted_iota(jnp.int32, (tq, LANES), 1)
    first = lane < head_dim
    zero = jnp.zeros_like(q)
    q_heads = (jnp.where(first, q, zero), jnp.where(first, zero, q))
    scale = head_dim ** -0.5
    u = u_ref[...]
    row = lax.broadcasted_iota(jnp.int32, (tq, tq), 0)
    col = lax.broadcasted_iota(jnp.int32, (tq, tq), 1)
    past = col < row

    def tile(j, carry, diagonal):
        run0, run1, acc = carry
        start = pl.multiple_of(j * tq, tq)
        kt = k_ref[pl.ds(start, tq), :]
        vt = v_ref[pl.ds(start, tq), :]
        new_runs = []
        pvs = []
        for qh, run in zip(q_heads, (run0, run1)):
            z = lax.dot_general(qh, kt, (((1,), (1,)), ((), ())), preferred_element_type=F32) * scale
            sp = jnp.maximum(z, 0.0) + jnp.log1p(jnp.exp(-jnp.abs(z)))
            log_not = -sp
            if diagonal:
                log_not = jnp.where(past, log_not, 0.0)
            hi = log_not.astype(BF16)
            lo = (log_not - hi.astype(F32)).astype(BF16)
            cs = (jnp.dot(hi, u, preferred_element_type=F32) + jnp.dot(lo, u, preferred_element_type=F32))
            later = cs[:, :tq] + jnp.concatenate([run, run], axis=1)
            a = jnp.exp((z - sp) + later)
            if diagonal:
                a = jnp.where(past, a, 0.0)
            pvs.append(jnp.dot(a.astype(BF16), vt, preferred_element_type=F32))
            new_runs.append(run + cs[:, tq:])
        acc = acc + jnp.where(first, pvs[0], pvs[1])
        return new_runs[0], new_runs[1], acc

    zeros = jnp.zeros((tq, LANES), F32)
    run0, run1, acc = tile(i, (zeros, zeros, zeros), True)

    def cond(c):
        j, rmax = c[0], c[1]
        return jnp.logical_and(j >= 0, rmax >= EXP_ZERO_BELOW)

    def body(c):
        j, _, r0, r1, ac = c
        r0, r1, ac = tile(j, (r0, r1, ac), False)
        return j - 1, jnp.max(jnp.maximum(r0, r1)), r0, r1, ac

    init = (i - 1, jnp.max(jnp.maximum(run0, run1)), run0, run1, acc)
    acc = lax.while_loop(cond, body, init)[4]
    o_ref[...] = acc.astype(o_ref.dtype)


def _sb_attention(proj, batch, seq, d_model):
    t = proj.shape[0]
    head_dim = d_model // SB_HEADS
    assert 2 * head_dim == LANES
    groups = d_model // LANES
    tq = SB_TILE
    nq = seq // tq
    j = np.arange(tq)
    strictly_later = (j[:, None] > j[None, :]).astype(np.float32)
    u = jnp.asarray(np.concatenate([strictly_later, np.ones((tq, LANES), np.float32)], axis=1), BF16)
    return pl.pallas_call(
        functools.partial(_sb_kernel, head_dim=head_dim),
        grid=(batch, groups, nq),
        in_specs=[
            pl.BlockSpec((tq, LANES), lambda b, p, i: (b * nq + i, p)),
            pl.BlockSpec((seq, LANES), lambda b, p, i: (b, groups + p)),
            pl.BlockSpec((seq, LANES), lambda b, p, i: (b, 2 * groups + p)),
            pl.BlockSpec((tq, tq + LANES), lambda b, p, i: (0, 0)),
        ],
        out_specs=pl.BlockSpec((tq, LANES), lambda b, p, i: (b * nq + i, p)),
        out_shape=jax.ShapeDtypeStruct((t, d_model), BF16),
        compiler_params=_cparams(("parallel", "parallel", "arbitrary")),
        name="stick_breaking",
    )(proj, proj, proj, u)


def _layer_norm(h, g, b):
    mu = jnp.mean(h, axis=-1, keepdims=True)
    var = jnp.mean(jnp.square(h - mu), axis=-1, keepdims=True)
    return (h - mu) * lax.rsqrt(var + LN_EPS) * g + b


def _split_bf16(x):
    hi = x.astype(BF16)
    return hi, (x - hi.astype(F32)).astype(BF16)


def _mix_router_kernel(a_ref, w_ref, x_ref, g_ref, b_ref, rw_ref, rb_ref, tri_ref,
                       x1_ref, idx_ref, gate_ref, cnt_ref, carry_ref, *, alpha):
    tm = a_ref.shape[0]

    @pl.when(pl.program_id(0) == 0)
    def _():
        carry_ref[...] = jnp.zeros_like(carry_ref)

    y = jnp.dot(a_ref[...], w_ref[...], preferred_element_type=F32)
    x1 = _layer_norm(alpha * x_ref[...] + y, g_ref[...], b_ref[...])
    x1_ref[...] = x1

    xh, xl = _split_bf16(x1)
    wh, wl = _split_bf16(rw_ref[...])
    logits = (jnp.dot(xh, wh, preferred_element_type=F32) + jnp.dot(xl, wh, preferred_element_type=F32)
              + jnp.dot(xh, wl, preferred_element_type=F32)) + rb_ref[...]

    lane = lax.broadcasted_iota(jnp.int32, (tm, LANES), 1)
    work = logits
    sel_idx, sel_val, onehots = [], [], []
    for _ in range(TOP_K):
        m = jnp.max(work, axis=-1, keepdims=True)
        sel = jnp.min(jnp.where(work == m, lane, LANES), axis=-1, keepdims=True)
        hit = lane == sel
        sel_idx.append(sel)
        sel_val.append(m)
        onehots.append(hit)
        work = jnp.where(hit, -jnp.inf, work)
    exps = [jnp.exp(v - sel_val[0]) for v in sel_val]
    denom = exps[0] + exps[1] + exps[2] + exps[3]

    member = jnp.zeros((tm, LANES), F32)
    for hit in onehots:
        member = member + jnp.where(hit, 1.0, 0.0)
    prefix = jnp.dot(tri_ref[...], member.astype(BF16), preferred_element_type=F32)
    base = carry_ref[0:1, :] + prefix
    idx_out = jnp.zeros((tm, LANES), jnp.int32)
    gate_out = jnp.zeros((tm, LANES), F32)
    for k in range(TOP_K):
        rank = jnp.sum(jnp.where(onehots[k], base, 0.0), axis=-1, keepdims=True).astype(jnp.int32)
        idx_out = jnp.where(lane == k, sel_idx[k], idx_out)
        idx_out = jnp.where(lane == TOP_K + k, rank, idx_out)
        gate_out = jnp.where(lane == k, exps[k] / denom, gate_out)
    idx_ref[...] = idx_out
    gate_ref[...] = gate_out
    carry_ref[...] = carry_ref[...] + jnp.sum(member, axis=0, keepdims=True)
    cnt_ref[...] = carry_ref[...]


def _mix_router(a, w_out, x, ln_g, ln_b, router_w, router_b, alpha, tm=512):
    t, kin = a.shape
    d = x.shape[1]
    e = router_w.shape[1]
    tm = min(tm, t)
    rw = jnp.pad(router_w, ((0, 0), (0, LANES - e)))
    rb = jnp.pad(router_b, (0, LANES - e), constant_values=-jnp.inf).reshape(1, LANES)
    r = np.arange(tm)
    tri = jnp.asarray((r[None, :] < r[:, None]).astype(np.float32), BF16)
    tile = lambda i: (i, 0)
    const = lambda i: (0, 0)
    return pl.pallas_call(
        functools.partial(_mix_router_kernel, alpha=alpha),
        grid=(t // tm,),
        in_specs=[
            pl.BlockSpec((tm, kin), tile),
            pl.BlockSpec((kin, d), const),
            pl.BlockSpec((tm, d), tile),
            pl.BlockSpec((1, d), const),
            pl.BlockSpec((1, d), const),
            pl.BlockSpec((d, LANES), const),
            pl.BlockSpec((1, LANES), const),
            pl.BlockSpec((tm, tm), const),
        ],
        out_specs=[
            pl.BlockSpec((tm, d), tile),
            pl.BlockSpec((tm, LANES), tile),
            pl.BlockSpec((tm, LANES), tile),
            pl.BlockSpec((8, LANES), const),
        ],
        out_shape=[
            jax.ShapeDtypeStruct((t, d), F32),
            jax.ShapeDtypeStruct((t, LANES), jnp.int32),
            jax.ShapeDtypeStruct((t, LANES), F32),
            jax.ShapeDtypeStruct((8, LANES), F32),
        ],
        scratch_shapes=[pltpu.VMEM((8, LANES), F32)],
        compiler_params=_cparams(("arbitrary",)),
        name="mix_ln_router",
    )(a, w_out, x, ln_g.reshape(1, d), ln_b.reshape(1, d), rw, rb, tri)


def _dispatch_kernel(dest_ref, x_ref, xs_in_ref, xs_ref, sem):
    del xs_in_ref
    tm = x_ref.shape[0]
    base = pl.program_id(0) * (tm * TOP_K)

    def row_copy(t, d):
        return pltpu.make_async_copy(x_ref.at[pl.ds(t, 1), :], xs_ref.at[pl.ds(d, 1), :], sem)

    def issue(t, c):
        for k in range(TOP_K):
            row_copy(t, dest_ref[base + t * TOP_K + k]).start()
        return c

    lax.fori_loop(0, tm, issue, 0)
    for k in range(TOP_K):
        pltpu.make_async_copy(x_ref, xs_ref.at[pl.ds(0, tm), :], sem).wait()


def _dispatch(dest, x1, n_rows, tm=512):
    t, d = x1.shape
    tm = min(tm, t)
    zeros = jnp.zeros((n_rows, d), x1.dtype)
    return pl.pallas_call(
        _dispatch_kernel,
        grid_spec=pltpu.PrefetchScalarGridSpec(
            num_scalar_prefetch=1,
            grid=(t // tm,),
            in_specs=[pl.BlockSpec((tm, d), lambda i, dest: (i, 0)),
                      pl.BlockSpec(memory_space=pl.ANY)],
            out_specs=pl.BlockSpec(memory_space=pl.ANY),
            scratch_shapes=[pltpu.SemaphoreType.DMA],
        ),
        out_shape=jax.ShapeDtypeStruct((n_rows, d), x1.dtype),
        input_output_aliases={2: 0},
        compiler_params=_cparams(("arbitrary",)),
        name="dispatch_rows",
    )(dest, x1, zeros)


def _experts_kernel(bstart_ref, bcount_ref, xs_ref, wgu_ref, bg_ref, bu_ref, wd_ref, bd_ref, perm_ref,
                    ys_ref, wg_s, wu_s, wd_s, xbuf, ybuf, xsem, ysem, *, n_blocks):
    e = pl.program_id(0)
    first = bstart_ref[e]
    n = bcount_ref[e]
    tm = ROW_BLOCK
    f = wd_ref.shape[0]

    def rows(j):
        return pl.ds(pl.multiple_of((first + j) * tm, tm), tm)

    def x_copy(j, slot):
        return pltpu.make_async_copy(xs_ref.at[rows(j), :], xbuf.at[slot], xsem.at[slot])

    def y_copy(j, slot):
        return pltpu.make_async_copy(ybuf.at[slot], ys_ref.at[rows(j), :], ysem.at[slot])

    @pl.when(n > 0)
    def _():
        x_copy(0, 0).start()
        half = MXU_DIM // 2
        for c in range(2 * f // MXU_DIM):
            blk = wgu_ref[:, c * MXU_DIM:(c + 1) * MXU_DIM].astype(BF16)
            sep = jnp.dot(blk, perm_ref[...], preferred_element_type=F32).astype(BF16)
            wg_s[:, c * half:(c + 1) * half] = sep[:, :half]
            wu_s[:, c * half:(c + 1) * half] = sep[:, half:]
        wd_s[...] = wd_ref[...].astype(BF16)

        def block(j, carry):
            slot = j % 2
            x_copy(j, slot).wait()

            @pl.when(j + 1 < n)
            def _():
                x_copy(j + 1, 1 - slot).start()

            @pl.when(j >= 2)
            def _():
                y_copy(j - 2, slot).wait()

            x = xbuf[slot].astype(BF16)
            gate = jnp.dot(x, wg_s[...], preferred_element_type=F32) + bg_ref[...]
            up = jnp.dot(x, wu_s[...], preferred_element_type=F32) + bu_ref[...]
            gate = jnp.minimum(gate, SWIGLU_LIMIT)
            up = jnp.clip(up, -SWIGLU_LIMIT, SWIGLU_LIMIT)
            act = (up + 1.0) * (gate * jax.nn.sigmoid(gate * SWIGLU_ALPHA))
            ybuf[slot] = jnp.dot(act.astype(BF16), wd_s[...], preferred_element_type=F32) + bd_ref[...]
            y_copy(j, slot).start()
            return carry

        lax.fori_loop(0, n, block, 0)

        @pl.when(n >= 2)
        def _():
            y_copy(n - 2, n % 2).wait()

        y_copy(n - 1, (n - 1) % 2).wait()

    @pl.when(e == pl.num_programs(0) - 1)
    def _():
        used = first + n

        @pl.when(used < n_blocks)
        def _():
            ybuf[0] = jnp.zeros((tm, ybuf.shape[2]), F32)

            def fill(j, carry):
                cp = pltpu.make_async_copy(ybuf.at[0], ys_ref.at[pl.ds(pl.multiple_of(j * tm, tm), tm), :], ysem.at[0])
                cp.start()
                cp.wait()
                return carry

            lax.fori_loop(used, n_blocks, fill, 0)


def _experts(block_first, block_count, xs, layer, w_gate_up, b_gate, b_up, w_down, b_down):
    n_rows, d = xs.shape
    _, n_exp, _, f2 = w_gate_up.shape
    f = f2 // 2
    tm = ROW_BLOCK
    c = np.arange(MXU_DIM)
    src = np.where(c < MXU_DIM // 2, 2 * c, 2 * (c - MXU_DIM // 2) + 1)
    perm = jnp.asarray((np.arange(MXU_DIM)[:, None] == src[None, :]).astype(np.float32), BF16)
    by_expert = lambda e, bf, bc: (e, 0, 0)
    by_layer_expert = lambda e, bf, bc: (layer, e, 0, 0)
    return pl.pallas_call(
        functools.partial(_experts_kernel, n_blocks=n_rows // tm),
        grid_spec=pltpu.PrefetchScalarGridSpec(
            num_scalar_prefetch=2,
            grid=(n_exp,),
            in_specs=[
                pl.BlockSpec(memory_space=pl.ANY),
                pl.BlockSpec((None, None, d, f2), by_layer_expert),
                pl.BlockSpec((None, 1, f), by_expert),
                pl.BlockSpec((None, 1, f), by_expert),
                pl.BlockSpec((None, None, f, d), by_layer_expert),
                pl.BlockSpec((None, 1, d), by_expert),
                pl.BlockSpec((MXU_DIM, MXU_DIM), lambda e, bf, bc: (0, 0)),
            ],
            out_specs=pl.BlockSpec(memory_space=pl.ANY),
            scratch_shapes=[pltpu.VMEM((d, f), BF16), pltpu.VMEM((d, f), BF16), pltpu.VMEM((f, d), BF16),
                            pltpu.VMEM((2, tm, d), F32), pltpu.VMEM((2, tm, d), F32),
                            pltpu.SemaphoreType.DMA((2,)), pltpu.SemaphoreType.DMA((2,))],
        ),
        out_shape=jax.ShapeDtypeStruct((n_rows, d), F32),
        compiler_params=_cparams(("arbitrary",)),
        name="expert_swiglu",
    )(block_first, block_count, xs, w_gate_up, b_gate.reshape(n_exp, 1, f), b_up.reshape(n_exp, 1, f),
      w_down, b_down.reshape(n_exp, 1, d), perm)


def _combine_kernel(dest_ref, ys_ref, gate_ref, x1_ref, g_ref, b_ref, pgw_ref, pgb_ref, p_ref, pw_ref,
                    out_ref, outb_ref, buf, sem, *, alpha):
    tm = x1_ref.shape[0]
    i = pl.program_id(0)
    slot = i % 2

    def gather(tile, into):
        base = tile * (tm * TOP_K)

        def issue(t, c):
            for k in range(TOP_K):
                d = dest_ref[base + t * TOP_K + k]
                pltpu.make_async_copy(ys_ref.at[pl.ds(d, 1), :], buf.at[into, k, pl.ds(t, 1), :], sem.at[into]).start()
            return c

        lax.fori_loop(0, tm, issue, 0)

    @pl.when(i == 0)
    def _():
        gather(0, 0)

    @pl.when(i + 1 < pl.num_programs(0))
    def _():
        gather(i + 1, 1 - slot)

    for k in range(TOP_K):
        pltpu.make_async_copy(ys_ref.at[pl.ds(0, tm), :], buf.at[slot, k], sem.at[slot]).wait()

    gates = gate_ref[...]
    ffn = gates[:, 0:1] * buf[slot, 0]
    for k in range(1, TOP_K):
        ffn = ffn + gates[:, k:k + 1] * buf[slot, k]
    x2 = _layer_norm(alpha * x1_ref[...] + ffn, g_ref[...], b_ref[...])
    ple_gate = jax.nn.sigmoid(jnp.dot(x2.astype(BF16), pgw_ref[...], preferred_element_type=F32) + pgb_ref[...])
    emb = jnp.dot(p_ref[...].astype(BF16), pw_ref[...], preferred_element_type=F32)
    out = x2 + ple_gate * emb
    out_ref[...] = out
    outb_ref[...] = out.astype(BF16)


def _combine(dest, ys, gates, x1, ln_g, ln_b, ple_gate_w, ple_gate_b, p, ple_w, alpha, tm=256):
    t, d = x1.shape
    dp = p.shape[1]
    tm = min(tm, t)
    tile = lambda i, dest: (i, 0)
    const = lambda i, dest: (0, 0)
    return pl.pallas_call(
        functools.partial(_combine_kernel, alpha=alpha),
        grid_spec=pltpu.PrefetchScalarGridSpec(
            num_scalar_prefetch=1,
            grid=(t // tm,),
            in_specs=[
                pl.BlockSpec(memory_space=pl.ANY),
                pl.BlockSpec((tm, LANES), tile),
                pl.BlockSpec((tm, d), tile),
                pl.BlockSpec((1, d), const),
                pl.BlockSpec((1, d), const),
                pl.BlockSpec((d, d), const),
                pl.BlockSpec((1, d), const),
                pl.BlockSpec((tm, dp), tile),
                pl.BlockSpec((dp, d), const),
            ],
            out_specs=[pl.BlockSpec((tm, d), tile), pl.BlockSpec((tm, d), tile)],
            scratch_shapes=[pltpu.VMEM((2, TOP_K, tm, d), F32), pltpu.SemaphoreType.DMA((2,))],
        ),
        out_shape=[jax.ShapeDtypeStruct((t, d), F32), jax.ShapeDtypeStruct((t, d), BF16)],
        compiler_params=_cparams(("arbitrary",)),
        name="combine_ln_ple",
    )(dest, ys, gates, x1, ln_g.reshape(1, d), ln_b.reshape(1, d), ple_gate_w, ple_gate_b.reshape(1, d), p, ple_w)


def _routing(idx_rank, counts, n_exp):
    top_idx = idx_rank[:, :TOP_K]
    rank = idx_rank[:, TOP_K:2 * TOP_K]
    counts = counts[0, :n_exp].astype(jnp.int32)
    padded = (counts + ROW_BLOCK - 1) // ROW_BLOCK * ROW_BLOCK
    pad_ends = jnp.cumsum(padded)
    pad_starts = pad_ends - padded
    experts = jnp.arange(n_exp, dtype=jnp.int32)
    start_of = jnp.sum(jnp.where(top_idx[:, :, None] == experts, pad_starts, 0), axis=-1)
    dest = (start_of + rank).reshape(-1).astype(jnp.int32)
    return dest, (pad_starts // ROW_BLOCK).astype(jnp.int32), (padded // ROW_BLOCK).astype(jnp.int32)


def kernel(x, p, ret_w_in, ret_w_out, sb_w_in, sb_w_out, ln1_g, ln1_b, router_w, router_b, w_gate_up, b_gate_up,
           w_down, b_down, ln2_g, ln2_b, ple_w, ple_gate_w, ple_gate_b):
    batch, seq, d = x.shape
    depth = ln1_g.shape[0]
    n_exp = router_w.shape[-1]
    t = batch * seq
    alpha = float((2 * depth) ** 0.25)
    ret_heads = d // RET_QK_DIM
    n_blocks = -(-(t * TOP_K + n_exp * (ROW_BLOCK - 1)) // ROW_BLOCK)
    n_rows = n_blocks * ROW_BLOCK

    xf = x.reshape(t, d)
    xb = xf.astype(BF16)
    for i in range(depth):
        j = i // 2
        if i % 2 == 0:
            proj = _matmul(xb, ret_w_in[j].astype(BF16), BF16)
            mixed = _retention(proj, batch, seq, ret_heads)
            w_out = ret_w_out[j].astype(BF16)
        else:
            proj = _matmul(xb, sb_w_in[j].astype(BF16), BF16)
            mixed = _sb_attention(proj, batch, seq, d)
            w_out = sb_w_out[j].astype(BF16)
        x1, idx_rank, gates, counts = _mix_router(mixed, w_out, xf, ln1_g[i], ln1_b[i], router_w[i], router_b[i], alpha)
        dest, block_first, block_count = _routing(idx_rank, counts, n_exp)
        xs = _dispatch(dest, x1, n_rows)
        ys = _experts(block_first, block_count, xs, i, w_gate_up, b_gate_up[i][:, 0::2], b_gate_up[i][:, 1::2],
                      w_down, b_down[i])
        xf, xb = _combine(dest, ys, gates, x1, ln2_g[i], ln2_b[i], ple_gate_w[i].astype(BF16), ple_gate_b[i],
                          p[i].reshape(t, -1), ple_w[i].astype(BF16), alpha)
    return xf.reshape(batch, seq, d)
```

```python
import functools

import jax
import jax.numpy as jnp
import numpy as np
from jax import lax
from jax.experimental import pallas as pl
from jax.experimental.pallas import tpu as pltpu

F32 = jnp.float32
BF16 = jnp.bfloat16

RET_QK_DIM = 256
RET_V_DIM = 512
ROPE_BASE = 10000.0
GN_EPS = 1e-6
LN_EPS = 1e-5
SB_HEADS = 16
TOP_K = 4
SWIGLU_LIMIT = 7.0
SWIGLU_ALPHA = 1.702

LANES = 128
MXU_DIM = 256
VMEM_LIMIT = 60 * 1024 * 1024

RET_BLOCK = 256
RET_CHUNK = 64
SB_TILE = 256
ROW_BLOCK = 256
EXP_ZERO_BELOW = -104.0


def _cparams(sem):
    return pltpu.CompilerParams(dimension_semantics=sem, vmem_limit_bytes=VMEM_LIMIT)


def _mm_kernel(x_ref, w_ref, o_ref):
    o_ref[...] = jnp.dot(x_ref[...], w_ref[...], preferred_element_type=F32).astype(o_ref.dtype)


def _matmul(x, w, out_dtype, tm=1024, tn=1024):
    m, k = x.shape
    n = w.shape[1]
    tm = min(tm, m)
    tn = min(tn, n)
    return pl.pallas_call(
        _mm_kernel,
        grid=(n // tn, m // tm),
        in_specs=[pl.BlockSpec((tm, k), lambda j, i: (i, 0)),
                  pl.BlockSpec((k, tn), lambda j, i: (0, j))],
        out_specs=pl.BlockSpec((tm, tn), lambda j, i: (i, j)),
        out_shape=jax.ShapeDtypeStruct((m, n), out_dtype),
        compiler_params=_cparams(("parallel", "parallel")),
        name="proj_matmul",
    )(x, w)


def _retention_kernel(q_ref, k_ref, v_ref, g_ref, cos_ref, sin_ref, dmat_ref, qdec_ref, kdec_ref,
                      cdec_ref, o_ref, state_ref):
    half = RET_QK_DIM // 2
    cos = cos_ref[...]
    sin = sin_ref[...]

    def rot(t):
        t1, t2 = t[:, :half], t[:, half:]
        return jnp.concatenate([t1 * cos - t2 * sin, t1 * sin + t2 * cos], axis=-1)

    q = rot(q_ref[...].astype(F32))
    k = rot(k_ref[...].astype(F32)) * (RET_QK_DIM ** -0.5)
    v = v_ref[...]

    @pl.when(pl.program_id(2) == 0)
    def _():
        state_ref[...] = jnp.zeros_like(state_ref)

    state = state_ref[...]
    scores = lax.dot_general(q.astype(BF16), k.astype(BF16), (((1,), (1,)), ((), ())),
                             preferred_element_type=F32) * dmat_ref[...]
    inner = jnp.dot(scores.astype(BF16), v, preferred_element_type=F32)
    cross = jnp.dot((q * qdec_ref[...]).astype(BF16), state.astype(BF16), preferred_element_type=F32)
    kv = lax.dot_general((k * kdec_ref[...]).astype(BF16), v, (((0,), (0,)), ((), ())),
                         preferred_element_type=F32)
    state_ref[...] = state * cdec_ref[...] + kv

    out = inner + cross
    mu = jnp.mean(out, axis=-1, keepdims=True)
    var = jnp.mean(jnp.square(out - mu), axis=-1, keepdims=True)
    normed = (out - mu) * lax.rsqrt(var + GN_EPS)
    g = g_ref[...].astype(F32)
    o_ref[...] = (g * jax.nn.sigmoid(g) * normed).astype(o_ref.dtype)


def _retention_tables(seq, heads):
    half = RET_QK_DIM // 2
    inv_freq = 1.0 / (ROPE_BASE ** (jnp.arange(half, dtype=F32) / half))
    ang = jnp.arange(seq, dtype=F32)[:, None] * inv_freq[None, :]
    log_gamma = jnp.log(1.0 - 2.0 ** (-5.0 - jnp.arange(heads, dtype=F32)))
    pos = jnp.arange(RET_BLOCK)
    diff = (pos[:, None] - pos[None, :]).astype(F32)
    same_chunk = (pos[:, None] // RET_CHUNK) == (pos[None, :] // RET_CHUNK)
    earlier_chunk = (pos[None, :] // RET_CHUNK) < (pos[:, None] // RET_CHUNK)
    lg = log_gamma[:, None, None]
    dmat = jnp.where(same_chunk[None], jnp.exp(lg * jnp.abs(diff)[None]),
                     jnp.where(earlier_chunk[None], jnp.exp(lg * diff[None]), 0.0))
    idx = pos.astype(F32)
    qdec = jnp.exp(log_gamma[:, None] * (idx + 1.0))[:, :, None]
    kdec = jnp.exp(log_gamma[:, None] * (RET_BLOCK - 1.0 - idx))[:, :, None]
    cdec = jnp.exp(log_gamma * RET_BLOCK)[:, None, None]
    return jnp.cos(ang), jnp.sin(ang), dmat, qdec, kdec, cdec


def _retention(proj, batch, seq, heads):
    t = proj.shape[0]
    blk = RET_BLOCK
    nblk = seq // blk
    dk, dv = RET_QK_DIM, RET_V_DIM
    cos, sin, dmat, qdec, kdec, cdec = _retention_tables(seq, heads)
    row = lambda b, h, i: b * nblk + i
    return pl.pallas_call(
        _retention_kernel,
        grid=(batch, heads, nblk),
        in_specs=[
            pl.BlockSpec((blk, dk), lambda b, h, i: (row(b, h, i), h)),
            pl.BlockSpec((blk, dk), lambda b, h, i: (row(b, h, i), heads + h)),
            pl.BlockSpec((blk, dv), lambda b, h, i: (row(b, h, i), heads + h)),
            pl.BlockSpec((blk, dv), lambda b, h, i: (row(b, h, i), 2 * heads + h)),
            pl.BlockSpec((blk, dk // 2), lambda b, h, i: (i, 0)),
            pl.BlockSpec((blk, dk // 2), lambda b, h, i: (i, 0)),
            pl.BlockSpec((None, blk, blk), lambda b, h, i: (h, 0, 0)),
            pl.BlockSpec((None, blk, 1), lambda b, h, i: (h, 0, 0)),
            pl.BlockSpec((None, blk, 1), lambda b, h, i: (h, 0, 0)),
            pl.BlockSpec((None, 1, 1), lambda b, h, i: (h, 0, 0)),
        ],
        out_specs=pl.BlockSpec((blk, dv), lambda b, h, i: (row(b, h, i), h)),
        out_shape=jax.ShapeDtypeStruct((t, heads * dv), BF16),
        scratch_shapes=[pltpu.VMEM((dk, dv), F32)],
        compiler_params=_cparams(("parallel", "parallel", "arbitrary")),
        name="retention",
    )(proj, proj, proj, proj, cos, sin, dmat, qdec, kdec, cdec)


def _sb_kernel(q_ref, k_ref, v_ref, u_ref, o_ref, *, head_dim):
    tq = SB_TILE
    i = pl.program_id(2)
    q = q_ref[...]
    lane = lax.broadcasted_iota(jnp.int32, (tq, LANES), 1)
    first = lane < head_dim
    zero = jnp.zeros_like(q)
    q_heads = (jnp.where(first, q, zero), jnp.where(first, zero, q))
    scale = head_dim ** -0.5
    u = u_ref[...]
    row = lax.broadcasted_iota(jnp.int32, (tq, tq), 0)
    col = lax.broadcasted_iota(jnp.int32, (tq, tq), 1)
    past = col < row

    def tile(j, carry, diagonal):
        run0, run1, acc = carry
        start = pl.multiple_of(j * tq, tq)
        kt = k_ref[pl.ds(start, tq), :]
        vt = v_ref[pl.ds(start, tq), :]
        new_runs = []
        pvs = []
        for qh, run in zip(q_heads, (run0, run1)):
            z = lax.dot_general(qh, kt, (((1,), (1,)), ((), ())), preferred_element_type=F32) * scale
            sp = jnp.maximum(z, 0.0) + jnp.log1p(jnp.exp(-jnp.abs(z)))
            log_not = -sp
            if diagonal:
                log_not = jnp.where(past, log_not, 0.0)
            hi = log_not.astype(BF16)
            lo = (log_not - hi.astype(F32)).astype(BF16)
            cs = (jnp.dot(hi, u, preferred_element_type=F32) + jnp.dot(lo, u, preferred_element_type=F32))
            later = cs[:, :tq] + jnp.concatenate([run, run], axis=1)
            a = jnp.exp((z - sp) + later)
            if diagonal:
                a = jnp.where(past, a, 0.0)
            pvs.append(jnp.dot(a.astype(BF16), vt, preferred_element_type=F32))
            new_runs.append(run + cs[:, tq:])
        acc = acc + jnp.where(first, pvs[0], pvs[1])
        return new_runs[0], new_runs[1], acc

    zeros = jnp.zeros((tq, LANES), F32)
    run0, run1, acc = tile(i, (zeros, zeros, zeros), True)

    def cond(c):
        j, rmax = c[0], c[1]
        return jnp.logical_and(j >= 0, rmax >= EXP_ZERO_BELOW)

    def body(c):
        j, _, r0, r1, ac = c
        r0, r1, ac = tile(j, (r0, r1, ac), False)
        return j - 1, jnp.max(jnp.maximum(r0, r1)), r0, r1, ac

    init = (i - 1, jnp.max(jnp.maximum(run0, run1)), run0, run1, acc)
    acc = lax.while_loop(cond, body, init)[4]
    o_ref[...] = acc.astype(o_ref.dtype)


def _sb_attention(proj, batch, seq, d_model):
    t = proj.shape[0]
    head_dim = d_model // SB_HEADS
    assert 2 * head_dim == LANES
    groups = d_model // LANES
    tq = SB_TILE
    nq = seq // tq
    j = np.arange(tq)
    strictly_later = (j[:, None] > j[None, :]).astype(np.float32)
    u = jnp.asarray(np.concatenate([strictly_later, np.ones((tq, LANES), np.float32)], axis=1), BF16)
    return pl.pallas_call(
        functools.partial(_sb_kernel, head_dim=head_dim),
        grid=(batch, groups, nq),
        in_specs=[
            pl.BlockSpec((tq, LANES), lambda b, p, i: (b * nq + i, p)),
            pl.BlockSpec((seq, LANES), lambda b, p, i: (b, groups + p)),
            pl.BlockSpec((seq, LANES), lambda b, p, i: (b, 2 * groups + p)),
            pl.BlockSpec((tq, tq + LANES), lambda b, p, i: (0, 0)),
        ],
        out_specs=pl.BlockSpec((tq, LANES), lambda b, p, i: (b * nq + i, p)),
        out_shape=jax.ShapeDtypeStruct((t, d_model), BF16),
        compiler_params=_cparams(("parallel", "parallel", "arbitrary")),
        name="stick_breaking",
    )(proj, proj, proj, u)


def _layer_norm(h, g, b):
    mu = jnp.mean(h, axis=-1, keepdims=True)
    var = jnp.mean(jnp.square(h - mu), axis=-1, keepdims=True)
    return (h - mu) * lax.rsqrt(var + LN_EPS) * g + b


def _split_bf16(x):
    hi = x.astype(BF16)
    return hi, (x - hi.astype(F32)).astype(BF16)


def _mix_router_kernel(a_ref, w_ref, x_ref, g_ref, b_ref, rw_ref, rb_ref, tri_ref,
                       x1_ref, idx_ref, gate_ref, cnt_ref, carry_ref, *, alpha):
    tm = a_ref.shape[0]

    @pl.when(pl.program_id(0) == 0)
    def _():
        carry_ref[...] = jnp.zeros_like(carry_ref)

    y = jnp.dot(a_ref[...], w_ref[...], preferred_element_type=F32)
    x1 = _layer_norm(alpha * x_ref[...] + y, g_ref[...], b_ref[...])
    x1_ref[...] = x1

    xh, xl = _split_bf16(x1)
    wh, wl = _split_bf16(rw_ref[...])
    logits = (jnp.dot(xh, wh, preferred_element_type=F32) + jnp.dot(xl, wh, preferred_element_type=F32)
              + jnp.dot(xh, wl, preferred_element_type=F32)) + rb_ref[...]

    lane = lax.broadcasted_iota(jnp.int32, (tm, LANES), 1)
    work = logits
    sel_idx, sel_val, onehots = [], [], []
    for _ in range(TOP_K):
        m = jnp.max(work, axis=-1, keepdims=True)
        sel = jnp.min(jnp.where(work == m, lane, LANES), axis=-1, keepdims=True)
        hit = lane == sel
        sel_idx.append(sel)
        sel_val.append(m)
        onehots.append(hit)
        work = jnp.where(hit, -jnp.inf, work)
    exps = [jnp.exp(v - sel_val[0]) for v in sel_val]
    denom = exps[0] + exps[1] + exps[2] + exps[3]

    member = jnp.zeros((tm, LANES), F32)
    for hit in onehots:
        member = member + jnp.where(hit, 1.0, 0.0)
    prefix = jnp.dot(tri_ref[...], member.astype(BF16), preferred_element_type=F32)
    base = carry_ref[0:1, :] + prefix
    idx_out = jnp.zeros((tm, LANES), jnp.int32)
    gate_out = jnp.zeros((tm, LANES), F32)
    for k in range(TOP_K):
        rank = jnp.sum(jnp.where(onehots[k], base, 0.0), axis=-1, keepdims=True).astype(jnp.int32)
        idx_out = jnp.where(lane == k, sel_idx[k], idx_out)
        idx_out = jnp.where(lane == TOP_K + k, rank, idx_out)
        gate_out = jnp.where(lane == k, exps[k] / denom, gate_out)
    idx_ref[...] = idx_out
    gate_ref[...] = gate_out
    carry_ref[...] = carry_ref[...] + jnp.sum(member, axis=0, keepdims=True)
    cnt_ref[...] = carry_ref[...]


def _mix_router(a, w_out, x, ln_g, ln_b, router_w, router_b, alpha, tm=512):
    t, kin = a.shape
    d = x.shape[1]
    e = router_w.shape[1]
    tm = min(tm, t)
    rw = jnp.pad(router_w, ((0, 0), (0, LANES - e)))
    rb = jnp.pad(router_b, (0, LANES - e), constant_values=-jnp.inf).reshape(1, LANES)
    r = np.arange(tm)
    tri = jnp.asarray((r[None, :] < r[:, None]).astype(np.float32), BF16)
    tile = lambda i: (i, 0)
    const = lambda i: (0, 0)
    return pl.pallas_call(
        functools.partial(_mix_router_kernel, alpha=alpha),
        grid=(t // tm,),
        in_specs=[
            pl.BlockSpec((tm, kin), tile),
            pl.BlockSpec((kin, d), const),
            pl.BlockSpec((tm, d), tile),
            pl.BlockSpec((1, d), const),
            pl.BlockSpec((1, d), const),
            pl.BlockSpec((d, LANES), const),
            pl.BlockSpec((1, LANES), const),
            pl.BlockSpec((tm, tm), const),
        ],
        out_specs=[
            pl.BlockSpec((tm, d), tile),
            pl.BlockSpec((tm, LANES), tile),
            pl.BlockSpec((tm, LANES), tile),
            pl.BlockSpec((8, LANES), const),
        ],
        out_shape=[
            jax.ShapeDtypeStruct((t, d), F32),
            jax.ShapeDtypeStruct((t, LANES), jnp.int32),
            jax.ShapeDtypeStruct((t, LANES), F32),
            jax.ShapeDtypeStruct((8, LANES), F32),
        ],
        scratch_shapes=[pltpu.VMEM((8, LANES), F32)],
        compiler_params=_cparams(("arbitrary",)),
        name="mix_ln_router",
    )(a, w_out, x, ln_g.reshape(1, d), ln_b.reshape(1, d), rw, rb, tri)


def _dispatch_kernel(dest_ref, pad_lo_ref, pad_hi_ref, x_ref, xs_ref, zrow, sem, zsem):
    tm = x_ref.shape[0]
    base = pl.program_id(0) * (tm * TOP_K)

    def row_copy(t, d):
        return pltpu.make_async_copy(x_ref.at[pl.ds(t, 1), :], xs_ref.at[pl.ds(d, 1), :], sem)

    def issue(t, c):
        for k in range(TOP_K):
            row_copy(t, dest_ref[base + t * TOP_K + k]).start()
        return c

    lax.fori_loop(0, tm, issue, 0)

    @pl.when(pl.program_id(0) == pl.num_programs(0) - 1)
    def _():
        zrow[...] = jnp.zeros_like(zrow)

        def zero_copy(r):
            return pltpu.make_async_copy(zrow.at[pl.ds(0, 1), :], xs_ref.at[pl.ds(r, 1), :], zsem)

        def per_expert(e, c):
            lax.fori_loop(pad_lo_ref[e], pad_hi_ref[e], lambda r, c2: (zero_copy(r).start(), c2)[1], 0)
            lax.fori_loop(pad_lo_ref[e], pad_hi_ref[e], lambda r, c2: (zero_copy(r).wait(), c2)[1], 0)
            return c

        lax.fori_loop(0, pad_lo_ref.shape[0], per_expert, 0)

        blk = zrow.shape[0]

        def zero_block(j, c):
            cp = pltpu.make_async_copy(zrow, xs_ref.at[pl.ds(pl.multiple_of(j * blk, blk), blk), :], zsem)
            cp.start()
            cp.wait()
            return c

        lax.fori_loop(pad_hi_ref[pad_hi_ref.shape[0] - 1] // blk, xs_ref.shape[0] // blk, zero_block, 0)

    for k in range(TOP_K):
        pltpu.make_async_copy(x_ref, xs_ref.at[pl.ds(0, tm), :], sem).wait()


def _dispatch(dest, pad_lo, pad_hi, x1, n_rows, tm=512):
    t, d = x1.shape
    tm = min(tm, t)
    return pl.pallas_call(
        _dispatch_kernel,
        grid_spec=pltpu.PrefetchScalarGridSpec(
            num_scalar_prefetch=3,
            grid=(t // tm,),
            in_specs=[pl.BlockSpec((tm, d), lambda i, *_: (i, 0))],
            out_specs=pl.BlockSpec(memory_space=pl.ANY),
            scratch_shapes=[pltpu.VMEM((ROW_BLOCK, d), x1.dtype), pltpu.SemaphoreType.DMA, pltpu.SemaphoreType.DMA],
        ),
        out_shape=jax.ShapeDtypeStruct((n_rows, d), x1.dtype),
        compiler_params=_cparams(("arbitrary",)),
        name="dispatch_rows",
    )(dest, pad_lo, pad_hi, x1)


def _experts_kernel(bstart_ref, bcount_ref, xs_ref, wgu_ref, bg_ref, bu_ref, wd_ref, bd_ref, perm_ref,
                    ys_ref, wg_s, wu_s, wd_s, xbuf, ybuf, xtail, ytail, xsem, ysem, tsem, *, n_blocks):
    e = pl.program_id(0)
    first = bstart_ref[e]
    n = bcount_ref[e]
    tm = ROW_BLOCK
    big = 2 * tm
    f = wd_ref.shape[0]
    n_big = n // 2
    has_tail = n % 2 == 1

    def big_rows(j):
        return pl.ds(pl.multiple_of((first + 2 * j) * tm, tm), big)

    def x_copy(j, slot):
        return pltpu.make_async_copy(xs_ref.at[big_rows(j), :], xbuf.at[slot], xsem.at[slot])

    def y_copy(j, slot):
        return pltpu.make_async_copy(ybuf.at[slot], ys_ref.at[big_rows(j), :], ysem.at[slot])

    tail_rows = pl.ds(pl.multiple_of((first + n - 1) * tm, tm), tm)
    xt_copy = pltpu.make_async_copy(xs_ref.at[tail_rows, :], xtail, tsem.at[0])
    yt_copy = pltpu.make_async_copy(ytail, ys_ref.at[tail_rows, :], tsem.at[1])

    def ffn(x):
        x = x.astype(BF16)
        gate = jnp.dot(x, wg_s[...], preferred_element_type=F32) + bg_ref[...]
        up = jnp.dot(x, wu_s[...], preferred_element_type=F32) + bu_ref[...]
        gate = jnp.minimum(gate, SWIGLU_LIMIT)
        up = jnp.clip(up, -SWIGLU_LIMIT, SWIGLU_LIMIT)
        act = (up + 1.0) * (gate * jax.nn.sigmoid(gate * SWIGLU_ALPHA))
        return jnp.dot(act.astype(BF16), wd_s[...], preferred_element_type=F32) + bd_ref[...]

    @pl.when(n > 0)
    def _():
        @pl.when(n_big > 0)
        def _():
            x_copy(0, 0).start()

        @pl.when(has_tail)
        def _():
            xt_copy.start()

        half = MXU_DIM // 2
        for c in range(2 * f // MXU_DIM):
            blk = wgu_ref[:, c * MXU_DIM:(c + 1) * MXU_DIM].astype(BF16)
            sep = jnp.dot(blk, perm_ref[...], preferred_element_type=F32).astype(BF16)
            wg_s[:, c * half:(c + 1) * half] = sep[:, :half]
            wu_s[:, c * half:(c + 1) * half] = sep[:, half:]
        wd_s[...] = wd_ref[...].astype(BF16)

        def step(j, carry):
            slot = j % 2
            x_copy(j, slot).wait()

            @pl.when(j + 1 < n_big)
            def _():
                x_copy(j + 1, 1 - slot).start()

            @pl.when(j >= 2)
            def _():
                y_copy(j - 2, slot).wait()

            ybuf[slot] = ffn(xbuf[slot])
            y_copy(j, slot).start()
            return carry

        lax.fori_loop(0, n_big, step, 0)

        @pl.when(has_tail)
        def _():
            xt_copy.wait()
            ytail[...] = ffn(xtail[...])
            yt_copy.start()

        @pl.when(n_big >= 2)
        def _():
            y_copy(n_big - 2, n_big % 2).wait()

        @pl.when(n_big >= 1)
        def _():
            y_copy(n_big - 1, (n_big - 1) % 2).wait()

        @pl.when(has_tail)
        def _():
            yt_copy.wait()

    @pl.when(e == pl.num_programs(0) - 1)
    def _():
        used = first + n

        @pl.when(used < n_blocks)
        def _():
            ytail[...] = jnp.zeros_like(ytail)

            def fill(j, carry):
                cp = pltpu.make_async_copy(ytail, ys_ref.at[pl.ds(pl.multiple_of(j * tm, tm), tm), :], tsem.at[1])
                cp.start()
                cp.wait()
                return carry

            lax.fori_loop(used, n_blocks, fill, 0)


def _experts(block_first, block_count, xs, layer, w_gate_up, b_gate, b_up, w_down, b_down):
    n_rows, d = xs.shape
    _, n_exp, _, f2 = w_gate_up.shape
    f = f2 // 2
    tm = ROW_BLOCK
    c = np.arange(MXU_DIM)
    src = np.where(c < MXU_DIM // 2, 2 * c, 2 * (c - MXU_DIM // 2) + 1)
    perm = jnp.asarray((np.arange(MXU_DIM)[:, None] == src[None, :]).astype(np.float32), BF16)
    by_expert = lambda e, bf, bc: (e, 0, 0)
    by_layer_expert = lambda e, bf, bc: (layer, e, 0, 0)
    return pl.pallas_call(
        functools.partial(_experts_kernel, n_blocks=n_rows // tm),
        grid_spec=pltpu.PrefetchScalarGridSpec(
            num_scalar_prefetch=2,
            grid=(n_exp,),
            in_specs=[
                pl.BlockSpec(memory_space=pl.ANY),
                pl.BlockSpec((None, None, d, f2), by_layer_expert),
                pl.BlockSpec((None, 1, f), by_expert),
                pl.BlockSpec((None, 1, f), by_expert),
                pl.BlockSpec((None, None, f, d), by_layer_expert),
                pl.BlockSpec((None, 1, d), by_expert),
                pl.BlockSpec((MXU_DIM, MXU_DIM), lambda e, bf, bc: (0, 0)),
            ],
            out_specs=pl.BlockSpec(memory_space=pl.ANY),
            scratch_shapes=[pltpu.VMEM((d, f), BF16), pltpu.VMEM((d, f), BF16), pltpu.VMEM((f, d), BF16),
                            pltpu.VMEM((2, 2 * tm, d), F32), pltpu.VMEM((2, 2 * tm, d), F32),
                            pltpu.VMEM((tm, d), F32), pltpu.VMEM((tm, d), F32),
                            pltpu.SemaphoreType.DMA((2,)), pltpu.SemaphoreType.DMA((2,)), pltpu.SemaphoreType.DMA((2,))],
        ),
        out_shape=jax.ShapeDtypeStruct((n_rows, d), F32),
        compiler_params=_cparams(("arbitrary",)),
        name="expert_swiglu",
    )(block_first, block_count, xs, w_gate_up, b_gate.reshape(n_exp, 1, f), b_up.reshape(n_exp, 1, f),
      w_down, b_down.reshape(n_exp, 1, d), perm)


def _combine_kernel(dest_ref, ys_ref, gate_ref, x1_ref, g_ref, b_ref, pgw_ref, pgb_ref, p_ref, pw_ref,
                    out_ref, outb_ref, buf, sem, *, alpha):
    tm = x1_ref.shape[0]
    i = pl.program_id(0)
    slot = i % 2

    def gather(tile, into):
        base = tile * (tm * TOP_K)

        def issue(t, c):
            for k in range(TOP_K):
                d = dest_ref[base + t * TOP_K + k]
                pltpu.make_async_copy(ys_ref.at[pl.ds(d, 1), :], buf.at[into, k, pl.ds(t, 1), :], sem.at[into]).start()
            return c

        lax.fori_loop(0, tm, issue, 0)

    @pl.when(i == 0)
    def _():
        gather(0, 0)

    @pl.when(i + 1 < pl.num_programs(0))
    def _():
        gather(i + 1, 1 - slot)

    for k in range(TOP_K):
        pltpu.make_async_copy(ys_ref.at[pl.ds(0, tm), :], buf.at[slot, k], sem.at[slot]).wait()

    gates = gate_ref[...]
    ffn = gates[:, 0:1] * buf[slot, 0]
    for k in range(1, TOP_K):
        ffn = ffn + gates[:, k:k + 1] * buf[slot, k]
    x2 = _layer_norm(alpha * x1_ref[...] + ffn, g_ref[...], b_ref[...])
    ple_gate = jax.nn.sigmoid(jnp.dot(x2.astype(BF16), pgw_ref[...], preferred_element_type=F32) + pgb_ref[...])
    emb = jnp.dot(p_ref[...].astype(BF16), pw_ref[...], preferred_element_type=F32)
    out = x2 + ple_gate * emb
    out_ref[...] = out
    outb_ref[...] = out.astype(BF16)


def _combine(dest, ys, gates, x1, ln_g, ln_b, ple_gate_w, ple_gate_b, p, ple_w, alpha, tm=256):
    t, d = x1.shape
    dp = p.shape[1]
    tm = min(tm, t)
    tile = lambda i, dest: (i, 0)
    const = lambda i, dest: (0, 0)
    return pl.pallas_call(
        functools.partial(_combine_kernel, alpha=alpha),
        grid_spec=pltpu.PrefetchScalarGridSpec(
            num_scalar_prefetch=1,
            grid=(t // tm,),
            in_specs=[
                pl.BlockSpec(memory_space=pl.ANY),
                pl.BlockSpec((tm, LANES), tile),
                pl.BlockSpec((tm, d), tile),
                pl.BlockSpec((1, d), const),
                pl.BlockSpec((1, d), const),
                pl.BlockSpec((d, d), const),
                pl.BlockSpec((1, d), const),
                pl.BlockSpec((tm, dp), tile),
                pl.BlockSpec((dp, d), const),
            ],
            out_specs=[pl.BlockSpec((tm, d), tile), pl.BlockSpec((tm, d), tile)],
            scratch_shapes=[pltpu.VMEM((2, TOP_K, tm, d), F32), pltpu.SemaphoreType.DMA((2,))],
        ),
        out_shape=[jax.ShapeDtypeStruct((t, d), F32), jax.ShapeDtypeStruct((t, d), BF16)],
        compiler_params=_cparams(("arbitrary",)),
        name="combine_ln_ple",
    )(dest, ys, gates, x1, ln_g.reshape(1, d), ln_b.reshape(1, d), ple_gate_w, ple_gate_b.reshape(1, d), p, ple_w)


def _routing(idx_rank, counts, n_exp):
    top_idx = idx_rank[:, :TOP_K]
    rank = idx_rank[:, TOP_K:2 * TOP_K]
    counts = counts[0, :n_exp].astype(jnp.int32)
    padded = (counts + ROW_BLOCK - 1) // ROW_BLOCK * ROW_BLOCK
    pad_ends = jnp.cumsum(padded)
    pad_starts = pad_ends - padded
    experts = jnp.arange(n_exp, dtype=jnp.int32)
    start_of = jnp.sum(jnp.where(top_idx[:, :, None] == experts, pad_starts, 0), axis=-1)
    dest = (start_of + rank).reshape(-1).astype(jnp.int32)
    blocks = ((pad_starts // ROW_BLOCK).astype(jnp.int32), (padded // ROW_BLOCK).astype(jnp.int32))
    pad_rows = ((pad_starts + counts).astype(jnp.int32), pad_ends.astype(jnp.int32))
    return dest, blocks, pad_rows


def kernel(x, p, ret_w_in, ret_w_out, sb_w_in, sb_w_out, ln1_g, ln1_b, router_w, router_b, w_gate_up, b_gate_up,
           w_down, b_down, ln2_g, ln2_b, ple_w, ple_gate_w, ple_gate_b):
    batch, seq, d = x.shape
    depth = ln1_g.shape[0]
    n_exp = router_w.shape[-1]
    t = batch * seq
    alpha = float((2 * depth) ** 0.25)
    ret_heads = d // RET_QK_DIM
    n_blocks = -(-(t * TOP_K + n_exp * (ROW_BLOCK - 1)) // ROW_BLOCK)
    n_rows = n_blocks * ROW_BLOCK

    xf = x.reshape(t, d)
    xb = xf.astype(BF16)
    for i in range(depth):
        j = i // 2
        if i % 2 == 0:
            proj = _matmul(xb, ret_w_in[j].astype(BF16), BF16)
            mixed = _retention(proj, batch, seq, ret_heads)
            w_out = ret_w_out[j].astype(BF16)
        else:
            proj = _matmul(xb, sb_w_in[j].astype(BF16), BF16)
            mixed = _sb_attention(proj, batch, seq, d)
            w_out = sb_w_out[j].astype(BF16)
        x1, idx_rank, gates, counts = _mix_router(mixed, w_out, xf, ln1_g[i], ln1_b[i], router_w[i], router_b[i], alpha)
        dest, (block_first, block_count), (pad_lo, pad_hi) = _routing(idx_rank, counts, n_exp)
        xs = _dispatch(dest, pad_lo, pad_hi, x1, n_rows)
        ys = _experts(block_first, block_count, xs, i, w_gate_up, b_gate_up[i][:, 0::2], b_gate_up[i][:, 1::2],
                      w_down, b_down[i])
        xf, xb = _combine(dest, ys, gates, x1, ln2_g[i], ln2_b[i], ple_gate_w[i].astype(BF16), ple_gate_b[i],
                          p[i].reshape(t, -1), ple_w[i].astype(BF16), alpha)
    return xf.reshape(batch, seq, d)
```

```python
import functools

import jax
import jax.numpy as jnp
import numpy as np
from jax import lax
from jax.experimental import pallas as pl
from jax.experimental.pallas import tpu as pltpu

F32 = jnp.float32
BF16 = jnp.bfloat16

RET_QK_DIM = 256
RET_V_DIM = 512
ROPE_BASE = 10000.0
GN_EPS = 1e-6
LN_EPS = 1e-5
SB_HEADS = 16
TOP_K = 4
SWIGLU_LIMIT = 7.0
SWIGLU_ALPHA = 1.702

LANES = 128
MXU_DIM = 256
VMEM_LIMIT = 60 * 1024 * 1024

RET_BLOCK = 256
RET_CHUNK = 64
SB_TILE = 256
ROW_BLOCK = 256
EXP_ZERO_BELOW = -104.0


def _cparams(sem):
    return pltpu.CompilerParams(dimension_semantics=sem, vmem_limit_bytes=VMEM_LIMIT)


def _mm_kernel(x_ref, w_ref, o_ref):
    o_ref[...] = jnp.dot(x_ref[...], w_ref[...], preferred_element_type=F32).astype(o_ref.dtype)


def _matmul(x, w, out_dtype, tm=1024, tn=1024):
    m, k = x.shape
    n = w.shape[1]
    tm = min(tm, m)
    tn = min(tn, n)
    return pl.pallas_call(
        _mm_kernel,
        grid=(n // tn, m // tm),
        in_specs=[pl.BlockSpec((tm, k), lambda j, i: (i, 0)),
                  pl.BlockSpec((k, tn), lambda j, i: (0, j))],
        out_specs=pl.BlockSpec((tm, tn), lambda j, i: (i, j)),
        out_shape=jax.ShapeDtypeStruct((m, n), out_dtype),
        compiler_params=_cparams(("parallel", "parallel")),
        name="proj_matmul",
    )(x, w)


def _retention_kernel(q_ref, k_ref, v_ref, g_ref, cos_ref, sin_ref, dmat_ref, qdec_ref, kdec_ref,
                      cdec_ref, o_ref, state_ref):
    half = RET_QK_DIM // 2
    cos = cos_ref[...]
    sin = sin_ref[...]

    def rot(t):
        t1, t2 = t[:, :half], t[:, half:]
        return jnp.concatenate([t1 * cos - t2 * sin, t1 * sin + t2 * cos], axis=-1)

    q = rot(q_ref[...].astype(F32))
    k = rot(k_ref[...].astype(F32)) * (RET_QK_DIM ** -0.5)
    v = v_ref[...]

    @pl.when(pl.program_id(2) == 0)
    def _():
        state_ref[...] = jnp.zeros_like(state_ref)

    state = state_ref[...]
    scores = lax.dot_general(q.astype(BF16), k.astype(BF16), (((1,), (1,)), ((), ())),
                             preferred_element_type=F32) * dmat_ref[...]
    inner = jnp.dot(scores.astype(BF16), v, preferred_element_type=F32)
    cross = jnp.dot((q * qdec_ref[...]).astype(BF16), state.astype(BF16), preferred_element_type=F32)
    kv = lax.dot_general((k * kdec_ref[...]).astype(BF16), v, (((0,), (0,)), ((), ())),
                         preferred_element_type=F32)
    state_ref[...] = state * cdec_ref[...] + kv

    out = inner + cross
    mu = jnp.mean(out, axis=-1, keepdims=True)
    var = jnp.mean(jnp.square(out - mu), axis=-1, keepdims=True)
    normed = (out - mu) * lax.rsqrt(var + GN_EPS)
    g = g_ref[...].astype(F32)
    o_ref[...] = (g * jax.nn.sigmoid(g) * normed).astype(o_ref.dtype)


def _retention_tables(seq, heads):
    half = RET_QK_DIM // 2
    inv_freq = 1.0 / (ROPE_BASE ** (jnp.arange(half, dtype=F32) / half))
    ang = jnp.arange(seq, dtype=F32)[:, None] * inv_freq[None, :]
    log_gamma = jnp.log(1.0 - 2.0 ** (-5.0 - jnp.arange(heads, dtype=F32)))
    pos = jnp.arange(RET_BLOCK)
    diff = (pos[:, None] - pos[None, :]).astype(F32)
    same_chunk = (pos[:, None] // RET_CHUNK) == (pos[None, :] // RET_CHUNK)
    earlier_chunk = (pos[None, :] // RET_CHUNK) < (pos[:, None] // RET_CHUNK)
    lg = log_gamma[:, None, None]
    dmat = jnp.where(same_chunk[None], jnp.exp(lg * jnp.abs(diff)[None]),
                     jnp.where(earlier_chunk[None], jnp.exp(lg * diff[None]), 0.0))
    idx = pos.astype(F32)
    qdec = jnp.exp(log_gamma[:, None] * (idx + 1.0))[:, :, None]
    kdec = jnp.exp(log_gamma[:, None] * (RET_BLOCK - 1.0 - idx))[:, :, None]
    cdec = jnp.exp(log_gamma * RET_BLOCK)[:, None, None]
    return jnp.cos(ang), jnp.sin(ang), dmat, qdec, kdec, cdec


def _retention(proj, batch, seq, heads):
    t = proj.shape[0]
    blk = RET_BLOCK
    nblk = seq // blk
    dk, dv = RET_QK_DIM, RET_V_DIM
    cos, sin, dmat, qdec, kdec, cdec = _retention_tables(seq, heads)
    row = lambda b, h, i: b * nblk + i
    return pl.pallas_call(
        _retention_kernel,
        grid=(batch, heads, nblk),
        in_specs=[
            pl.BlockSpec((blk, dk), lambda b, h, i: (row(b, h, i), h)),
            pl.BlockSpec((blk, dk), lambda b, h, i: (row(b, h, i), heads + h)),
            pl.BlockSpec((blk, dv), lambda b, h, i: (row(b, h, i), heads + h)),
            pl.BlockSpec((blk, dv), lambda b, h, i: (row(b, h, i), 2 * heads + h)),
            pl.BlockSpec((blk, dk // 2), lambda b, h, i: (i, 0)),
            pl.BlockSpec((blk, dk // 2), lambda b, h, i: (i, 0)),
            pl.BlockSpec((None, blk, blk), lambda b, h, i: (h, 0, 0)),
            pl.BlockSpec((None, blk, 1), lambda b, h, i: (h, 0, 0)),
            pl.BlockSpec((None, blk, 1), lambda b, h, i: (h, 0, 0)),
            pl.BlockSpec((None, 1, 1), lambda b, h, i: (h, 0, 0)),
        ],
        out_specs=pl.BlockSpec((blk, dv), lambda b, h, i: (row(b, h, i), h)),
        out_shape=jax.ShapeDtypeStruct((t, heads * dv), BF16),
        scratch_shapes=[pltpu.VMEM((dk, dv), F32)],
        compiler_params=_cparams(("parallel", "parallel", "arbitrary")),
        name="retention",
    )(proj, proj, proj, proj, cos, sin, dmat, qdec, kdec, cdec)


def _sb_kernel(q_ref, k_ref, v_ref, u_ref, o_ref, *, head_dim):
    tq = SB_TILE
    i = pl.program_id(2)
    q = q_ref[...]
    lane = lax.broadcasted_iota(jnp.int32, (tq, LANES), 1)
    first = lane < head_dim
    zero = jnp.zeros_like(q)
    q_heads = (jnp.where(first, q, zero), jnp.where(first, zero, q))
    scale = head_dim ** -0.5
    u = u_ref[...]
    row = lax.broadcasted_iota(jnp.int32, (tq, tq), 0)
    col = lax.broadcasted_iota(jnp.int32, (tq, tq), 1)
    past = col < row

    def tile(j, carry, diagonal):
        run0, run1, acc = carry
        start = pl.multiple_of(j * tq, tq)
        kt = k_ref[pl.ds(start, tq), :]
        vt = v_ref[pl.ds(start, tq), :]
        new_runs = []
        pvs = []
        for qh, run in zip(q_heads, (run0, run1)):
            z = lax.dot_general(qh, kt, (((1,), (1,)), ((), ())), preferred_element_type=F32) * scale
            sp = jnp.maximum(z, 0.0) + jnp.log1p(jnp.exp(-jnp.abs(z)))
            log_not = -sp
            if diagonal:
                log_not = jnp.where(past, log_not, 0.0)
            hi = log_not.astype(BF16)
            lo = (log_not - hi.astype(F32)).astype(BF16)
            cs = (jnp.dot(hi, u, preferred_element_type=F32) + jnp.dot(lo, u, preferred_element_type=F32))
            later = cs[:, :tq] + jnp.concatenate([run, run], axis=1)
            a = jnp.exp((z - sp) + later)
            if diagonal:
                a = jnp.where(past, a, 0.0)
            pvs.append(jnp.dot(a.astype(BF16), vt, preferred_element_type=F32))
            new_runs.append(run + cs[:, tq:])
        acc = acc + jnp.where(first, pvs[0], pvs[1])
        return new_runs[0], new_runs[1], acc

    zeros = jnp.zeros((tq, LANES), F32)
    run0, run1, acc = tile(i, (zeros, zeros, zeros), True)

    def cond(c):
        j, rmax = c[0], c[1]
        return jnp.logical_and(j >= 0, rmax >= EXP_ZERO_BELOW)

    def body(c):
        j, _, r0, r1, ac = c
        r0, r1, ac = tile(j, (r0, r1, ac), False)
        return j - 1, jnp.max(jnp.maximum(r0, r1)), r0, r1, ac

    init = (i - 1, jnp.max(jnp.maximum(run0, run1)), run0, run1, acc)
    acc = lax.while_loop(cond, body, init)[4]
    o_ref[...] = acc.astype(o_ref.dtype)


def _sb_attention(proj, batch, seq, d_model):
    t = proj.shape[0]
    head_dim = d_model // SB_HEADS
    assert 2 * head_dim == LANES
    groups = d_model // LANES
    tq = SB_TILE
    nq = seq // tq
    j = np.arange(tq)
    strictly_later = (j[:, None] > j[None, :]).astype(np.float32)
    u = jnp.asarray(np.concatenate([strictly_later, np.ones((tq, LANES), np.float32)], axis=1), BF16)
    return pl.pallas_call(
        functools.partial(_sb_kernel, head_dim=head_dim),
        grid=(batch, groups, nq),
        in_specs=[
            pl.BlockSpec((tq, LANES), lambda b, p, i: (b * nq + i, p)),
            pl.BlockSpec((seq, LANES), lambda b, p, i: (b, groups + p)),
            pl.BlockSpec((seq, LANES), lambda b, p, i: (b, 2 * groups + p)),
            pl.BlockSpec((tq, tq + LANES), lambda b, p, i: (0, 0)),
        ],
        out_specs=pl.BlockSpec((tq, LANES), lambda b, p, i: (b * nq + i, p)),
        out_shape=jax.ShapeDtypeStruct((t, d_model), BF16),
        compiler_params=_cparams(("parallel", "parallel", "arbitrary")),
        name="stick_breaking",
    )(proj, proj, proj, u)


def _layer_norm(h, g, b):
    mu = jnp.mean(h, axis=-1, keepdims=True)
    var = jnp.mean(jnp.square(h - mu), axis=-1, keepdims=True)
    return (h - mu) * lax.rsqrt(var + LN_EPS) * g + b


def _split_bf16(x):
    hi = x.astype(BF16)
    return hi, (x - hi.astype(F32)).astype(BF16)


def _mix_router_kernel(a_ref, w_ref, x_ref, g_ref, b_ref, rw_ref, rb_ref, tri_ref,
                       x1_ref, idx_ref, gate_ref, cnt_ref, carry_ref, *, alpha):
    tm = a_ref.shape[0]

    @pl.when(pl.program_id(0) == 0)
    def _():
        carry_ref[...] = jnp.zeros_like(carry_ref)

    y = jnp.dot(a_ref[...], w_ref[...], preferred_element_type=F32)
    x1 = _layer_norm(alpha * x_ref[...] + y, g_ref[...], b_ref[...])
    x1_ref[...] = x1

    xh, xl = _split_bf16(x1)
    wh, wl = _split_bf16(rw_ref[...])
    logits = (jnp.dot(xh, wh, preferred_element_type=F32) + jnp.dot(xl, wh, preferred_element_type=F32)
              + jnp.dot(xh, wl, preferred_element_type=F32)) + rb_ref[...]

    lane = lax.broadcasted_iota(jnp.int32, (tm, LANES), 1)
    work = logits
    sel_idx, sel_val, onehots = [], [], []
    for _ in range(TOP_K):
        m = jnp.max(work, axis=-1, keepdims=True)
        sel = jnp.min(jnp.where(work == m, lane, LANES), axis=-1, keepdims=True)
        hit = lane == sel
        sel_idx.append(sel)
        sel_val.append(m)
        onehots.append(hit)
        work = jnp.where(hit, -jnp.inf, work)
    exps = [jnp.exp(v - sel_val[0]) for v in sel_val]
    denom = exps[0] + exps[1] + exps[2] + exps[3]

    member = jnp.zeros((tm, LANES), F32)
    for hit in onehots:
        member = member + jnp.where(hit, 1.0, 0.0)
    prefix = jnp.dot(tri_ref[...], member.astype(BF16), preferred_element_type=F32)
    base = carry_ref[0:1, :] + prefix
    idx_out = jnp.zeros((tm, LANES), jnp.int32)
    gate_out = jnp.zeros((tm, LANES), F32)
    for k in range(TOP_K):
        rank = jnp.sum(jnp.where(onehots[k], base, 0.0), axis=-1, keepdims=True).astype(jnp.int32)
        idx_out = jnp.where(lane == k, sel_idx[k], idx_out)
        idx_out = jnp.where(lane == TOP_K + k, rank, idx_out)
        gate_out = jnp.where(lane == k, exps[k] / denom, gate_out)
    idx_ref[...] = idx_out
    gate_ref[...] = gate_out
    carry_ref[...] = carry_ref[...] + jnp.sum(member, axis=0, keepdims=True)
    cnt_ref[...] = carry_ref[...]


def _mix_router(a, w_out, x, ln_g, ln_b, router_w, router_b, alpha, tm=512):
    t, kin = a.shape
    d = x.shape[1]
    e = router_w.shape[1]
    tm = min(tm, t)
    rw = jnp.pad(router_w, ((0, 0), (0, LANES - e)))
    rb = jnp.pad(router_b, (0, LANES - e), constant_values=-jnp.inf).reshape(1, LANES)
    r = np.arange(tm)
    tri = jnp.asarray((r[None, :] < r[:, None]).astype(np.float32), BF16)
    tile = lambda i: (i, 0)
    const = lambda i: (0, 0)
    return pl.pallas_call(
        functools.partial(_mix_router_kernel, alpha=alpha),
        grid=(t // tm,),
        in_specs=[
            pl.BlockSpec((tm, kin), tile),
            pl.BlockSpec((kin, d), const),
            pl.BlockSpec((tm, d), tile),
            pl.BlockSpec((1, d), const),
            pl.BlockSpec((1, d), const),
            pl.BlockSpec((d, LANES), const),
            pl.BlockSpec((1, LANES), const),
            pl.BlockSpec((tm, tm), const),
        ],
        out_specs=[
            pl.BlockSpec((tm, d), tile),
            pl.BlockSpec((tm, LANES), tile),
            pl.BlockSpec((tm, LANES), tile),
            pl.BlockSpec((8, LANES), const),
        ],
        out_shape=[
            jax.ShapeDtypeStruct((t, d), F32),
            jax.ShapeDtypeStruct((t, LANES), jnp.int32),
            jax.ShapeDtypeStruct((t, LANES), F32),
            jax.ShapeDtypeStruct((8, LANES), F32),
        ],
        scratch_shapes=[pltpu.VMEM((8, LANES), F32)],
        compiler_params=_cparams(("arbitrary",)),
        name="mix_ln_router",
    )(a, w_out, x, ln_g.reshape(1, d), ln_b.reshape(1, d), rw, rb, tri)


def _dispatch_kernel(dest_ref, pad_lo_ref, pad_hi_ref, x_ref, xs_ref, inv_ref, zrow, sem, zsem):
    tm = x_ref.shape[0]
    tile0 = pl.program_id(0) * tm
    base = tile0 * TOP_K
    n_tokens = pl.num_programs(0) * tm

    def issue(t, c):
        for k in range(TOP_K):
            d = dest_ref[base + t * TOP_K + k]
            inv_ref[d] = k * n_tokens + tile0 + t
            pltpu.make_async_copy(x_ref.at[pl.ds(t, 1), :], xs_ref.at[pl.ds(d, 1), :], sem).start()
        return c

    lax.fori_loop(0, tm, issue, 0)

    @pl.when(pl.program_id(0) == pl.num_programs(0) - 1)
    def _():
        zrow[...] = jnp.zeros_like(zrow)
        n_exp = pad_lo_ref.shape[0]
        blk = zrow.shape[0]
        first_free = pad_hi_ref[n_exp - 1] // blk
        n_blk = xs_ref.shape[0] // blk

        def zero_row(r):
            return pltpu.make_async_copy(zrow.at[pl.ds(0, 1), :], xs_ref.at[pl.ds(r, 1), :], zsem)

        def zero_block(j):
            return pltpu.make_async_copy(zrow, xs_ref.at[pl.ds(pl.multiple_of(j * blk, blk), blk), :], zsem)

        def start_rows(e, c):
            def one(r, c2):
                inv_ref[r] = -1
                zero_row(r).start()
                return c2
            return lax.fori_loop(pad_lo_ref[e], pad_hi_ref[e], one, c)

        def wait_rows(e, c):
            return lax.fori_loop(pad_lo_ref[e], pad_hi_ref[e], lambda r, c2: (zero_row(r).wait(), c2)[1], c)

        lax.fori_loop(0, n_exp, start_rows, 0)
        lax.fori_loop(first_free, n_blk, lambda j, c: (zero_block(j).start(), c)[1], 0)

        def no_pair(r, c):
            inv_ref[r] = -1
            return c

        lax.fori_loop(first_free * blk, n_blk * blk, no_pair, 0)
        lax.fori_loop(0, n_exp, wait_rows, 0)
        lax.fori_loop(first_free, n_blk, lambda j, c: (zero_block(j).wait(), c)[1], 0)

    for k in range(TOP_K):
        pltpu.make_async_copy(x_ref, xs_ref.at[pl.ds(0, tm), :], sem).wait()


def _dispatch(dest, pad_lo, pad_hi, x1, n_rows, tm=512):
    t, d = x1.shape
    tm = min(tm, t)
    return pl.pallas_call(
        _dispatch_kernel,
        grid_spec=pltpu.PrefetchScalarGridSpec(
            num_scalar_prefetch=3,
            grid=(t // tm,),
            in_specs=[pl.BlockSpec((tm, d), lambda i, *_: (i, 0))],
            out_specs=[pl.BlockSpec(memory_space=pl.ANY), pl.BlockSpec(memory_space=pltpu.SMEM)],
            scratch_shapes=[pltpu.VMEM((ROW_BLOCK, d), x1.dtype), pltpu.SemaphoreType.DMA, pltpu.SemaphoreType.DMA],
        ),
        out_shape=[jax.ShapeDtypeStruct((n_rows, d), x1.dtype), jax.ShapeDtypeStruct((n_rows,), jnp.int32)],
        compiler_params=_cparams(("arbitrary",)),
        name="dispatch_rows",
    )(dest, pad_lo, pad_hi, x1)


def _experts_kernel(bstart_ref, bcount_ref, inv_ref, xs_ref, wgu_ref, bg_ref, bu_ref, wd_ref, bd_ref, perm_ref,
                    yt_ref, wg_s, wu_s, wd_s, xbuf, ybuf, xsem, ysem, *, n_pairs):
    e = pl.program_id(0)
    first = bstart_ref[e]
    n = bcount_ref[e]
    tm = ROW_BLOCK
    f = wd_ref.shape[0]

    def x_copy(j, slot):
        rows = pl.ds(pl.multiple_of((first + j) * tm, tm), tm)
        return pltpu.make_async_copy(xs_ref.at[rows, :], xbuf.at[slot], xsem.at[slot])

    def scatter_rows(j, slot):
        row0 = (first + j) * tm
        for r in range(tm):
            pair = inv_ref[row0 + r]
            dst = jnp.where(pair < 0, n_pairs + slot * tm + r, pair)
            pltpu.make_async_copy(ybuf.at[slot, pl.ds(r, 1), :], yt_ref.at[pl.ds(dst, 1), :], ysem.at[slot]).start()

    def rows_wait(slot):
        pltpu.make_async_copy(ybuf.at[slot], yt_ref.at[pl.ds(0, tm), :], ysem.at[slot]).wait()

    def ffn(x):
        x = x.astype(BF16)
        gate = jnp.dot(x, wg_s[...], preferred_element_type=F32) + bg_ref[...]
        up = jnp.dot(x, wu_s[...], preferred_element_type=F32) + bu_ref[...]
        gate = jnp.minimum(gate, SWIGLU_LIMIT)
        up = jnp.clip(up, -SWIGLU_LIMIT, SWIGLU_LIMIT)
        act = (up + 1.0) * (gate * jax.nn.sigmoid(gate * SWIGLU_ALPHA))
        return jnp.dot(act.astype(BF16), wd_s[...], preferred_element_type=F32) + bd_ref[...]

    @pl.when(e == 0)
    def _():
        ybuf[...] = jnp.zeros_like(ybuf)
        for slot in range(2):
            cp = pltpu.make_async_copy(ybuf.at[slot], yt_ref.at[pl.ds(n_pairs + slot * tm, tm), :], ysem.at[slot])
            cp.start()
            cp.wait()

    @pl.when(n > 0)
    def _():
        x_copy(0, 0).start()
        half = MXU_DIM // 2
        for c in range(2 * f // MXU_DIM):
            blk = wgu_ref[:, c * MXU_DIM:(c + 1) * MXU_DIM].astype(BF16)
            sep = jnp.dot(blk, perm_ref[...], preferred_element_type=F32).astype(BF16)
            wg_s[:, c * half:(c + 1) * half] = sep[:, :half]
            wu_s[:, c * half:(c + 1) * half] = sep[:, half:]
        wd_s[...] = wd_ref[...].astype(BF16)

        x_copy(0, 0).wait()

        @pl.when(n > 1)
        def _():
            x_copy(1, 1).start()

        ybuf[0] = ffn(xbuf[0])

        def step(j, carry):
            slot = j % 2
            x_copy(j, slot).wait()

            @pl.when(j + 1 < n)
            def _():
                x_copy(j + 1, 1 - slot).start()

            @pl.when(j >= 2)
            def _():
                rows_wait(slot)

            scatter_rows(j - 1, 1 - slot)
            ybuf[slot] = ffn(xbuf[slot])
            return carry

        lax.fori_loop(1, n, step, 0)

        last = (n - 1) % 2
        scatter_rows(n - 1, last)

        @pl.when(n >= 2)
        def _():
            rows_wait(1 - last)

        rows_wait(last)


def _experts(block_first, block_count, inv, xs, layer, n_pairs, w_gate_up, b_gate, b_up, w_down, b_down):
    n_rows, d = xs.shape
    _, n_exp, _, f2 = w_gate_up.shape
    f = f2 // 2
    tm = ROW_BLOCK
    c = np.arange(MXU_DIM)
    src = np.where(c < MXU_DIM // 2, 2 * c, 2 * (c - MXU_DIM // 2) + 1)
    perm = jnp.asarray((np.arange(MXU_DIM)[:, None] == src[None, :]).astype(np.float32), BF16)
    by_expert = lambda e, *_: (e, 0, 0)
    by_layer_expert = lambda e, *_: (layer, e, 0, 0)
    return pl.pallas_call(
        functools.partial(_experts_kernel, n_pairs=n_pairs),
        grid_spec=pltpu.PrefetchScalarGridSpec(
            num_scalar_prefetch=3,
            grid=(n_exp,),
            in_specs=[
                pl.BlockSpec(memory_space=pl.ANY),
                pl.BlockSpec((None, None, d, f2), by_layer_expert),
                pl.BlockSpec((None, 1, f), by_expert),
                pl.BlockSpec((None, 1, f), by_expert),
                pl.BlockSpec((None, None, f, d), by_layer_expert),
                pl.BlockSpec((None, 1, d), by_expert),
                pl.BlockSpec((MXU_DIM, MXU_DIM), lambda e, *_: (0, 0)),
            ],
            out_specs=pl.BlockSpec(memory_space=pl.ANY),
            scratch_shapes=[pltpu.VMEM((d, f), BF16), pltpu.VMEM((d, f), BF16), pltpu.VMEM((f, d), BF16),
                            pltpu.VMEM((2, tm, d), F32), pltpu.VMEM((2, tm, d), F32),
                            pltpu.SemaphoreType.DMA((2,)), pltpu.SemaphoreType.DMA((2,))],
        ),
        out_shape=jax.ShapeDtypeStruct((n_pairs + 2 * tm, d), F32),
        compiler_params=_cparams(("arbitrary",)),
        name="expert_swiglu",
    )(block_first, block_count, inv, xs, w_gate_up, b_gate.reshape(n_exp, 1, f), b_up.reshape(n_exp, 1, f),
      w_down, b_down.reshape(n_exp, 1, d), perm)


def _combine_kernel(y0_ref, y1_ref, y2_ref, y3_ref, gate_ref, x1_ref, g_ref, b_ref, pgw_ref, pgb_ref, p_ref, pw_ref,
                    out_ref, outb_ref, *, alpha):
    gates = gate_ref[...]
    ffn = gates[:, 0:1] * y0_ref[...]
    for k, y_ref in ((1, y1_ref), (2, y2_ref), (3, y3_ref)):
        ffn = ffn + gates[:, k:k + 1] * y_ref[...]
    x2 = _layer_norm(alpha * x1_ref[...] + ffn, g_ref[...], b_ref[...])
    ple_gate = jax.nn.sigmoid(jnp.dot(x2.astype(BF16), pgw_ref[...], preferred_element_type=F32) + pgb_ref[...])
    emb = jnp.dot(p_ref[...].astype(BF16), pw_ref[...], preferred_element_type=F32)
    out = x2 + ple_gate * emb
    out_ref[...] = out
    outb_ref[...] = out.astype(BF16)


def _combine(yt, gates, x1, ln_g, ln_b, ple_gate_w, ple_gate_b, p, ple_w, alpha, tm=256):
    t, d = x1.shape
    dp = p.shape[1]
    tm = min(tm, t)
    tile = lambda i: (i, 0)
    const = lambda i: (0, 0)
    return pl.pallas_call(
        functools.partial(_combine_kernel, alpha=alpha),
        grid=(t // tm,),
        in_specs=[
            pl.BlockSpec((tm, d), lambda i: (i, 0)),
            pl.BlockSpec((tm, d), lambda i: (t // tm + i, 0)),
            pl.BlockSpec((tm, d), lambda i: (2 * (t // tm) + i, 0)),
            pl.BlockSpec((tm, d), lambda i: (3 * (t // tm) + i, 0)),
            pl.BlockSpec((tm, LANES), tile),
            pl.BlockSpec((tm, d), tile),
            pl.BlockSpec((1, d), const),
            pl.BlockSpec((1, d), const),
            pl.BlockSpec((d, d), const),
            pl.BlockSpec((1, d), const),
            pl.BlockSpec((tm, dp), tile),
            pl.BlockSpec((dp, d), const),
        ],
        out_specs=[pl.BlockSpec((tm, d), tile), pl.BlockSpec((tm, d), tile)],
        out_shape=[jax.ShapeDtypeStruct((t, d), F32), jax.ShapeDtypeStruct((t, d), BF16)],
        compiler_params=_cparams(("parallel",)),
        name="combine_ln_ple",
    )(yt, yt, yt, yt, gates, x1, ln_g.reshape(1, d), ln_b.reshape(1, d), ple_gate_w, ple_gate_b.reshape(1, d), p, ple_w)


def _routing(idx_rank, counts, n_exp):
    top_idx = idx_rank[:, :TOP_K]
    rank = idx_rank[:, TOP_K:2 * TOP_K]
    counts = counts[0, :n_exp].astype(jnp.int32)
    padded = (counts + ROW_BLOCK - 1) // ROW_BLOCK * ROW_BLOCK
    pad_ends = jnp.cumsum(padded)
    pad_starts = pad_ends - padded
    experts = jnp.arange(n_exp, dtype=jnp.int32)
    start_of = jnp.sum(jnp.where(top_idx[:, :, None] == experts, pad_starts, 0), axis=-1)
    dest = (start_of + rank).reshape(-1).astype(jnp.int32)
    blocks = ((pad_starts // ROW_BLOCK).astype(jnp.int32), (padded // ROW_BLOCK).astype(jnp.int32))
    pad_rows = ((pad_starts + counts).astype(jnp.int32), pad_ends.astype(jnp.int32))
    return dest, blocks, pad_rows


def kernel(x, p, ret_w_in, ret_w_out, sb_w_in, sb_w_out, ln1_g, ln1_b, router_w, router_b, w_gate_up, b_gate_up,
           w_down, b_down, ln2_g, ln2_b, ple_w, ple_gate_w, ple_gate_b):
    batch, seq, d = x.shape
    depth = ln1_g.shape[0]
    n_exp = router_w.shape[-1]
    t = batch * seq
    alpha = float((2 * depth) ** 0.25)
    ret_heads = d // RET_QK_DIM
    n_blocks = -(-(t * TOP_K + n_exp * (ROW_BLOCK - 1)) // ROW_BLOCK)
    n_rows = n_blocks * ROW_BLOCK

    xf = x.reshape(t, d)
    xb = xf.astype(BF16)
    for i in range(depth):
        j = i // 2
        if i % 2 == 0:
            proj = _matmul(xb, ret_w_in[j].astype(BF16), BF16)
            mixed = _retention(proj, batch, seq, ret_heads)
            w_out = ret_w_out[j].astype(BF16)
        else:
            proj = _matmul(xb, sb_w_in[j].astype(BF16), BF16)
            mixed = _sb_attention(proj, batch, seq, d)
            w_out = sb_w_out[j].astype(BF16)
        x1, idx_rank, gates, counts = _mix_router(mixed, w_out, xf, ln1_g[i], ln1_b[i], router_w[i], router_b[i], alpha)
        dest, (block_first, block_count), (pad_lo, pad_hi) = _routing(idx_rank, counts, n_exp)
        xs, inv = _dispatch(dest, pad_lo, pad_hi, x1, n_rows)
        yt = _experts(block_first, block_count, inv, xs, i, t * TOP_K, w_gate_up, b_gate_up[i][:, 0::2],
                      b_gate_up[i][:, 1::2], w_down, b_down[i])
        xf, xb = _combine(yt, gates, x1, ln2_g[i], ln2_b[i], ple_gate_w[i].astype(BF16), ple_gate_b[i],
                          p[i].reshape(t, -1), ple_w[i].astype(BF16), alpha)
    return xf.reshape(batch, seq, d)
```

```python
import functools

import jax
import jax.numpy as jnp
import numpy as np
from jax import lax
from jax.experimental import pallas as pl
from jax.experimental.pallas import tpu as pltpu

F32 = jnp.float32
BF16 = jnp.bfloat16

RET_QK_DIM = 256
RET_V_DIM = 512
ROPE_BASE = 10000.0
GN_EPS = 1e-6
LN_EPS = 1e-5
SB_HEADS = 16
TOP_K = 4
SWIGLU_LIMIT = 7.0
SWIGLU_ALPHA = 1.702

LANES = 128
MXU_DIM = 256
VMEM_LIMIT = 60 * 1024 * 1024

RET_BLOCK = 256
RET_CHUNK = 64
SB_TILE = 256
ROW_BLOCK = 256
EXP_ZERO_BELOW = -104.0


def _cparams(sem):
    return pltpu.CompilerParams(dimension_semantics=sem, vmem_limit_bytes=VMEM_LIMIT)


def _mm_kernel(x_ref, w_ref, o_ref):
    o_ref[...] = jnp.dot(x_ref[...], w_ref[...], preferred_element_type=F32).astype(o_ref.dtype)


def _matmul(x, w, out_dtype, tm=1024, tn=1024):
    m, k = x.shape
    n = w.shape[1]
    tm = min(tm, m)
    tn = min(tn, n)
    return pl.pallas_call(
        _mm_kernel,
        grid=(n // tn, m // tm),
        in_specs=[pl.BlockSpec((tm, k), lambda j, i: (i, 0)),
                  pl.BlockSpec((k, tn), lambda j, i: (0, j))],
        out_specs=pl.BlockSpec((tm, tn), lambda j, i: (i, j)),
        out_shape=jax.ShapeDtypeStruct((m, n), out_dtype),
        compiler_params=_cparams(("parallel", "parallel")),
        name="proj_matmul",
    )(x, w)


def _retention_kernel(q_ref, k_ref, v_ref, g_ref, cos_ref, sin_ref, dmat_ref, qdec_ref, kdec_ref,
                      cdec_ref, o_ref, state_ref):
    half = RET_QK_DIM // 2
    cos = cos_ref[...]
    sin = sin_ref[...]

    def rot(t):
        t1, t2 = t[:, :half], t[:, half:]
        return jnp.concatenate([t1 * cos - t2 * sin, t1 * sin + t2 * cos], axis=-1)

    q = rot(q_ref[...].astype(F32))
    k = rot(k_ref[...].astype(F32)) * (RET_QK_DIM ** -0.5)
    v = v_ref[...]

    @pl.when(pl.program_id(2) == 0)
    def _():
        state_ref[...] = jnp.zeros_like(state_ref)

    state = state_ref[...]
    scores = lax.dot_general(q.astype(BF16), k.astype(BF16), (((1,), (1,)), ((), ())),
                             preferred_element_type=F32) * dmat_ref[...]
    inner = jnp.dot(scores.astype(BF16), v, preferred_element_type=F32)
    cross = jnp.dot((q * qdec_ref[...]).astype(BF16), state.astype(BF16), preferred_element_type=F32)
    kv = lax.dot_general((k * kdec_ref[...]).astype(BF16), v, (((0,), (0,)), ((), ())),
                         preferred_element_type=F32)
    state_ref[...] = state * cdec_ref[...] + kv

    out = inner + cross
    mu = jnp.mean(out, axis=-1, keepdims=True)
    var = jnp.mean(jnp.square(out - mu), axis=-1, keepdims=True)
    normed = (out - mu) * lax.rsqrt(var + GN_EPS)
    g = g_ref[...].astype(F32)
    o_ref[...] = (g * jax.nn.sigmoid(g) * normed).astype(o_ref.dtype)


def _retention_tables(seq, heads):
    half = RET_QK_DIM // 2
    inv_freq = 1.0 / (ROPE_BASE ** (jnp.arange(half, dtype=F32) / half))
    ang = jnp.arange(seq, dtype=F32)[:, None] * inv_freq[None, :]
    log_gamma = jnp.log(1.0 - 2.0 ** (-5.0 - jnp.arange(heads, dtype=F32)))
    pos = jnp.arange(RET_BLOCK)
    diff = (pos[:, None] - pos[None, :]).astype(F32)
    same_chunk = (pos[:, None] // RET_CHUNK) == (pos[None, :] // RET_CHUNK)
    earlier_chunk = (pos[None, :] // RET_CHUNK) < (pos[:, None] // RET_CHUNK)
    lg = log_gamma[:, None, None]
    dmat = jnp.where(same_chunk[None], jnp.exp(lg * jnp.abs(diff)[None]),
                     jnp.where(earlier_chunk[None], jnp.exp(lg * diff[None]), 0.0))
    idx = pos.astype(F32)
    qdec = jnp.exp(log_gamma[:, None] * (idx + 1.0))[:, :, None]
    kdec = jnp.exp(log_gamma[:, None] * (RET_BLOCK - 1.0 - idx))[:, :, None]
    cdec = jnp.exp(log_gamma * RET_BLOCK)[:, None, None]
    return jnp.cos(ang), jnp.sin(ang), dmat, qdec, kdec, cdec


def _retention(proj, batch, seq, heads):
    t = proj.shape[0]
    blk = RET_BLOCK
    nblk = seq // blk
    dk, dv = RET_QK_DIM, RET_V_DIM
    cos, sin, dmat, qdec, kdec, cdec = _retention_tables(seq, heads)
    row = lambda b, h, i: b * nblk + i
    return pl.pallas_call(
        _retention_kernel,
        grid=(batch, heads, nblk),
        in_specs=[
            pl.BlockSpec((blk, dk), lambda b, h, i: (row(b, h, i), h)),
            pl.BlockSpec((blk, dk), lambda b, h, i: (row(b, h, i), heads + h)),
            pl.BlockSpec((blk, dv), lambda b, h, i: (row(b, h, i), heads + h)),
            pl.BlockSpec((blk, dv), lambda b, h, i: (row(b, h, i), 2 * heads + h)),
            pl.BlockSpec((blk, dk // 2), lambda b, h, i: (i, 0)),
            pl.BlockSpec((blk, dk // 2), lambda b, h, i: (i, 0)),
            pl.BlockSpec((None, blk, blk), lambda b, h, i: (h, 0, 0)),
            pl.BlockSpec((None, blk, 1), lambda b, h, i: (h, 0, 0)),
            pl.BlockSpec((None, blk, 1), lambda b, h, i: (h, 0, 0)),
            pl.BlockSpec((None, 1, 1), lambda b, h, i: (h, 0, 0)),
        ],
        out_specs=pl.BlockSpec((blk, dv), lambda b, h, i: (row(b, h, i), h)),
        out_shape=jax.ShapeDtypeStruct((t, heads * dv), BF16),
        scratch_shapes=[pltpu.VMEM((dk, dv), F32)],
        compiler_params=_cparams(("parallel", "parallel", "arbitrary")),
        name="retention",
    )(proj, proj, proj, proj, cos, sin, dmat, qdec, kdec, cdec)


def _sb_kernel(q_ref, k_ref, v_ref, u_ref, o_ref, *, head_dim):
    tq = SB_TILE
    n_q = q_ref.shape[0] // tq
    lane = lax.broadcasted_iota(jnp.int32, (tq, LANES), 1)
    first = lane < head_dim
    scale = jnp.asarray(head_dim ** -0.5, BF16)
    u = u_ref[...]
    row = lax.broadcasted_iota(jnp.int32, (tq, tq), 0)
    col = lax.broadcasted_iota(jnp.int32, (tq, tq), 1)
    past = col < row

    def tile(q_heads, j, carry, diagonal):
        run0, run1, acc = carry
        start = pl.multiple_of(j * tq, tq)
        kt = k_ref[pl.ds(start, tq), :]
        vt = v_ref[pl.ds(start, tq), :]
        new_runs = []
        pvs = []
        for qh, run in zip(q_heads, (run0, run1)):
            z = lax.dot_general(qh, kt, (((1,), (1,)), ((), ())), preferred_element_type=F32)
            sp = jnp.maximum(z, 0.0) + jnp.log(1.0 + jnp.exp(-jnp.abs(z)))
            masked = jnp.where(past, sp, 0.0) if diagonal else sp
            hi = masked.astype(BF16)
            lo = (masked - hi.astype(F32)).astype(BF16)
            cs = jnp.dot(jnp.concatenate([hi, lo], axis=1), u, preferred_element_type=F32)
            a = jnp.exp(z - sp - cs[:, :tq] - jnp.concatenate([run, run], axis=1))
            if diagonal:
                a = jnp.where(past, a, 0.0)
            pvs.append(jnp.dot(a.astype(BF16), vt, preferred_element_type=F32))
            new_runs.append(run + cs[:, tq:])
        acc = acc + jnp.where(first, pvs[0], pvs[1])
        return new_runs[0], new_runs[1], acc

    def q_tile(i, carry):
        rows = pl.ds(pl.multiple_of(i * tq, tq), tq)
        q = q_ref[rows, :] * scale
        zero = jnp.zeros_like(q)
        q_heads = (jnp.where(first, q, zero), jnp.where(first, zero, q))
        zeros = jnp.zeros((tq, LANES), F32)
        run0, run1, acc = tile(q_heads, i, (zeros, zeros, zeros), True)

        def cond(c):
            return jnp.logical_and(c[0] >= 0, c[1] <= -EXP_ZERO_BELOW)

        def body(c):
            j, _, r0, r1, ac = c
            r0, r1, ac = tile(q_heads, j, (r0, r1, ac), False)
            return j - 1, jnp.min(jnp.minimum(r0, r1)), r0, r1, ac

        init = (i - 1, jnp.min(jnp.minimum(run0, run1)), run0, run1, acc)
        acc = lax.while_loop(cond, body, init)[4]
        o_ref[rows, :] = acc.astype(o_ref.dtype)
        return carry

    lax.fori_loop(0, n_q, q_tile, 0)


def _sb_attention(proj, batch, seq, d_model):
    t = proj.shape[0]
    head_dim = d_model // SB_HEADS
    assert 2 * head_dim == LANES
    groups = d_model // LANES
    tq = SB_TILE
    j = np.arange(tq)
    strictly_later = (j[:, None] > j[None, :]).astype(np.float32)
    u1 = np.concatenate([strictly_later, np.ones((tq, LANES), np.float32)], axis=1)
    u = jnp.asarray(np.concatenate([u1, u1], axis=0), BF16)
    return pl.pallas_call(
        functools.partial(_sb_kernel, head_dim=head_dim),
        grid=(batch, groups),
        in_specs=[
            pl.BlockSpec((seq, LANES), lambda b, p: (b, p)),
            pl.BlockSpec((seq, LANES), lambda b, p: (b, groups + p)),
            pl.BlockSpec((seq, LANES), lambda b, p: (b, 2 * groups + p)),
            pl.BlockSpec((2 * tq, tq + LANES), lambda b, p: (0, 0)),
        ],
        out_specs=pl.BlockSpec((seq, LANES), lambda b, p: (b, p)),
        out_shape=jax.ShapeDtypeStruct((t, d_model), BF16),
        compiler_params=_cparams(("parallel", "parallel")),
        name="stick_breaking",
    )(proj, proj, proj, u)


def _layer_norm(h, g, b):
    mu = jnp.mean(h, axis=-1, keepdims=True)
    var = jnp.mean(jnp.square(h - mu), axis=-1, keepdims=True)
    return (h - mu) * lax.rsqrt(var + LN_EPS) * g + b


def _split_bf16(x):
    hi = x.astype(BF16)
    return hi, (x - hi.astype(F32)).astype(BF16)


def _mix_router_kernel(a_ref, w_ref, x_ref, g_ref, b_ref, rw_ref, rb_ref, tri_ref,
                       x1_ref, idx_ref, gate_ref, cnt_ref, carry_ref, *, alpha):
    tm = a_ref.shape[0]

    @pl.when(pl.program_id(0) == 0)
    def _():
        carry_ref[...] = jnp.zeros_like(carry_ref)

    y = jnp.dot(a_ref[...], w_ref[...], preferred_element_type=F32)
    x1 = _layer_norm(alpha * x_ref[...] + y, g_ref[...], b_ref[...])
    x1_ref[...] = x1

    xh, xl = _split_bf16(x1)
    wh, wl = _split_bf16(rw_ref[...])
    logits = (jnp.dot(xh, wh, preferred_element_type=F32) + jnp.dot(xl, wh, preferred_element_type=F32)
              + jnp.dot(xh, wl, preferred_element_type=F32)) + rb_ref[...]

    lane = lax.broadcasted_iota(jnp.int32, (tm, LANES), 1)
    work = logits
    sel_idx, sel_val, onehots = [], [], []
    for _ in range(TOP_K):
        m = jnp.max(work, axis=-1, keepdims=True)
        sel = jnp.min(jnp.where(work == m, lane, LANES), axis=-1, keepdims=True)
        hit = lane == sel
        sel_idx.append(sel)
        sel_val.append(m)
        onehots.append(hit)
        work = jnp.where(hit, -jnp.inf, work)
    exps = [jnp.exp(v - sel_val[0]) for v in sel_val]
    denom = exps[0] + exps[1] + exps[2] + exps[3]

    member = jnp.zeros((tm, LANES), F32)
    for hit in onehots:
        member = member + jnp.where(hit, 1.0, 0.0)
    prefix = jnp.dot(tri_ref[...], member.astype(BF16), preferred_element_type=F32)
    base = carry_ref[0:1, :] + prefix
    idx_out = jnp.zeros((tm, LANES), jnp.int32)
    gate_out = jnp.zeros((tm, LANES), F32)
    for k in range(TOP_K):
        rank = jnp.sum(jnp.where(onehots[k], base, 0.0), axis=-1, keepdims=True).astype(jnp.int32)
        idx_out = jnp.where(lane == k, sel_idx[k], idx_out)
        idx_out = jnp.where(lane == TOP_K + k, rank, idx_out)
        gate_out = jnp.where(lane == k, exps[k] / denom, gate_out)
    idx_ref[...] = idx_out
    gate_ref[...] = gate_out
    carry_ref[...] = carry_ref[...] + jnp.sum(member, axis=0, keepdims=True)
    cnt_ref[...] = carry_ref[...]


def _mix_router(a, w_out, x, ln_g, ln_b, router_w, router_b, alpha, tm=512):
    t, kin = a.shape
    d = x.shape[1]
    e = router_w.shape[1]
    tm = min(tm, t)
    rw = jnp.pad(router_w, ((0, 0), (0, LANES - e)))
    rb = jnp.pad(router_b, (0, LANES - e), constant_values=-jnp.inf).reshape(1, LANES)
    r = np.arange(tm)
    tri = jnp.asarray((r[None, :] < r[:, None]).astype(np.float32), BF16)
    tile = lambda i: (i, 0)
    const = lambda i: (0, 0)
    return pl.pallas_call(
        functools.partial(_mix_router_kernel, alpha=alpha),
        grid=(t // tm,),
        in_specs=[
            pl.BlockSpec((tm, kin), tile),
            pl.BlockSpec((kin, d), const),
            pl.BlockSpec((tm, d), tile),
            pl.BlockSpec((1, d), const),
            pl.BlockSpec((1, d), const),
            pl.BlockSpec((d, LANES), const),
            pl.BlockSpec((1, LANES), const),
            pl.BlockSpec((tm, tm), const),
        ],
        out_specs=[
            pl.BlockSpec((tm, d), tile),
            pl.BlockSpec((tm, LANES), tile),
            pl.BlockSpec((tm, LANES), tile),
            pl.BlockSpec((8, LANES), const),
        ],
        out_shape=[
            jax.ShapeDtypeStruct((t, d), F32),
            jax.ShapeDtypeStruct((t, LANES), jnp.int32),
            jax.ShapeDtypeStruct((t, LANES), F32),
            jax.ShapeDtypeStruct((8, LANES), F32),
        ],
        scratch_shapes=[pltpu.VMEM((8, LANES), F32)],
        compiler_params=_cparams(("arbitrary",)),
        name="mix_ln_router",
    )(a, w_out, x, ln_g.reshape(1, d), ln_b.reshape(1, d), rw, rb, tri)


def _dispatch_kernel(dest_ref, pad_lo_ref, pad_hi_ref, x_ref, xs_ref, zrow, sem, zsem):
    tm = x_ref.shape[0]
    base = pl.program_id(0) * (tm * TOP_K)

    def row_copy(t, d):
        return pltpu.make_async_copy(x_ref.at[pl.ds(t, 1), :], xs_ref.at[pl.ds(d, 1), :], sem)

    def issue(t, c):
        for k in range(TOP_K):
            row_copy(t, dest_ref[base + t * TOP_K + k]).start()
        return c

    lax.fori_loop(0, tm, issue, 0)

    @pl.when(pl.program_id(0) == pl.num_programs(0) - 1)
    def _():
        zrow[...] = jnp.zeros_like(zrow)
        n_exp = pad_lo_ref.shape[0]
        blk = zrow.shape[0]
        first_free = pad_hi_ref[n_exp - 1] // blk
        n_blk = xs_ref.shape[0] // blk

        def zero_row(r):
            return pltpu.make_async_copy(zrow.at[pl.ds(0, 1), :], xs_ref.at[pl.ds(r, 1), :], zsem)

        def zero_block(j):
            return pltpu.make_async_copy(zrow, xs_ref.at[pl.ds(pl.multiple_of(j * blk, blk), blk), :], zsem)

        def start_rows(e, c):
            return lax.fori_loop(pad_lo_ref[e], pad_hi_ref[e], lambda r, c2: (zero_row(r).start(), c2)[1], c)

        def wait_rows(e, c):
            return lax.fori_loop(pad_lo_ref[e], pad_hi_ref[e], lambda r, c2: (zero_row(r).wait(), c2)[1], c)

        lax.fori_loop(0, n_exp, start_rows, 0)
        lax.fori_loop(first_free, n_blk, lambda j, c: (zero_block(j).start(), c)[1], 0)
        lax.fori_loop(0, n_exp, wait_rows, 0)
        lax.fori_loop(first_free, n_blk, lambda j, c: (zero_block(j).wait(), c)[1], 0)

    for k in range(TOP_K):
        pltpu.make_async_copy(x_ref, xs_ref.at[pl.ds(0, tm), :], sem).wait()


def _dispatch(dest, pad_lo, pad_hi, x1, n_rows, tm=512):
    t, d = x1.shape
    tm = min(tm, t)
    return pl.pallas_call(
        _dispatch_kernel,
        grid_spec=pltpu.PrefetchScalarGridSpec(
            num_scalar_prefetch=3,
            grid=(t // tm,),
            in_specs=[pl.BlockSpec((tm, d), lambda i, *_: (i, 0))],
            out_specs=pl.BlockSpec(memory_space=pl.ANY),
            scratch_shapes=[pltpu.VMEM((ROW_BLOCK, d), x1.dtype), pltpu.SemaphoreType.DMA, pltpu.SemaphoreType.DMA],
        ),
        out_shape=jax.ShapeDtypeStruct((n_rows, d), x1.dtype),
        compiler_params=_cparams(("arbitrary",)),
        name="dispatch_rows",
    )(dest, pad_lo, pad_hi, x1)


def _experts_kernel(bstart_ref, bcount_ref, xs_ref, wgu_ref, bg_ref, bu_ref, wd_ref, bd_ref, perm_ref,
                    ys_ref, wg_s, wu_s, wd_s, xbuf, ybuf, xtail, ytail, xsem, ysem, tsem, *, n_blocks):
    e = pl.program_id(0)
    first = bstart_ref[e]
    n = bcount_ref[e]
    tm = ROW_BLOCK
    big = 2 * tm
    f = wd_ref.shape[0]
    n_big = n // 2
    has_tail = n % 2 == 1

    def big_rows(j):
        return pl.ds(pl.multiple_of((first + 2 * j) * tm, tm), big)

    def x_copy(j, slot):
        return pltpu.make_async_copy(xs_ref.at[big_rows(j), :], xbuf.at[slot], xsem.at[slot])

    def y_copy(j, slot):
        return pltpu.make_async_copy(ybuf.at[slot], ys_ref.at[big_rows(j), :], ysem.at[slot])

    tail_rows = pl.ds(pl.multiple_of((first + n - 1) * tm, tm), tm)
    xt_copy = pltpu.make_async_copy(xs_ref.at[tail_rows, :], xtail, tsem.at[0])
    yt_copy = pltpu.make_async_copy(ytail, ys_ref.at[tail_rows, :], tsem.at[1])

    def ffn(x):
        x = x.astype(BF16)
        gate = jnp.dot(x, wg_s[...], preferred_element_type=F32) + bg_ref[...]
        up = jnp.dot(x, wu_s[...], preferred_element_type=F32) + bu_ref[...]
        gate = jnp.minimum(gate, SWIGLU_LIMIT)
        up = jnp.clip(up, -SWIGLU_LIMIT, SWIGLU_LIMIT)
        act = (up + 1.0) * (gate * jax.nn.sigmoid(gate * SWIGLU_ALPHA))
        return jnp.dot(act.astype(BF16), wd_s[...], preferred_element_type=F32) + bd_ref[...]

    @pl.when(n > 0)
    def _():
        @pl.when(n_big > 0)
        def _():
            x_copy(0, 0).start()

        @pl.when(has_tail)
        def _():
            xt_copy.start()

        half = MXU_DIM // 2
        for c in range(2 * f // MXU_DIM):
            blk = wgu_ref[:, c * MXU_DIM:(c + 1) * MXU_DIM].astype(BF16)
            sep = jnp.dot(blk, perm_ref[...], preferred_element_type=F32).astype(BF16)
            wg_s[:, c * half:(c + 1) * half] = sep[:, :half]
            wu_s[:, c * half:(c + 1) * half] = sep[:, half:]
        wd_s[...] = wd_ref[...].astype(BF16)

        def step(j, carry):
            slot = j % 2
            x_copy(j, slot).wait()

            @pl.when(j + 1 < n_big)
            def _():
                x_copy(j + 1, 1 - slot).start()

            @pl.when(j >= 2)
            def _():
                y_copy(j - 2, slot).wait()

            ybuf[slot] = ffn(xbuf[slot])
            y_copy(j, slot).start()
            return carry

        lax.fori_loop(0, n_big, step, 0)

        @pl.when(has_tail)
        def _():
            xt_copy.wait()
            ytail[...] = ffn(xtail[...])
            yt_copy.start()

        @pl.when(n_big >= 2)
        def _():
            y_copy(n_big - 2, n_big % 2).wait()

        @pl.when(n_big >= 1)
        def _():
            y_copy(n_big - 1, (n_big - 1) % 2).wait()

        @pl.when(has_tail)
        def _():
            yt_copy.wait()

    @pl.when(e == pl.num_programs(0) - 1)
    def _():
        used = first + n

        @pl.when(used < n_blocks)
        def _():
            ytail[...] = jnp.zeros_like(ytail)

            def fill(j, carry):
                cp = pltpu.make_async_copy(ytail, ys_ref.at[pl.ds(pl.multiple_of(j * tm, tm), tm), :], tsem.at[1])
                cp.start()
                cp.wait()
                return carry

            lax.fori_loop(used, n_blocks, fill, 0)


def _experts(block_first, block_count, xs, layer, w_gate_up, b_gate, b_up, w_down, b_down):
    n_rows, d = xs.shape
    _, n_exp, _, f2 = w_gate_up.shape
    f = f2 // 2
    tm = ROW_BLOCK
    c = np.arange(MXU_DIM)
    src = np.where(c < MXU_DIM // 2, 2 * c, 2 * (c - MXU_DIM // 2) + 1)
    perm = jnp.asarray((np.arange(MXU_DIM)[:, None] == src[None, :]).astype(np.float32), BF16)
    by_expert = lambda e, bf, bc: (e, 0, 0)
    by_layer_expert = lambda e, bf, bc: (layer, e, 0, 0)
    return pl.pallas_call(
        functools.partial(_experts_kernel, n_blocks=n_rows // tm),
        grid_spec=pltpu.PrefetchScalarGridSpec(
            num_scalar_prefetch=2,
            grid=(n_exp,),
            in_specs=[
                pl.BlockSpec(memory_space=pl.ANY),
                pl.BlockSpec((None, None, d, f2), by_layer_expert),
                pl.BlockSpec((None, 1, f), by_expert),
                pl.BlockSpec((None, 1, f), by_expert),
                pl.BlockSpec((None, None, f, d), by_layer_expert),
                pl.BlockSpec((None, 1, d), by_expert),
                pl.BlockSpec((MXU_DIM, MXU_DIM), lambda e, bf, bc: (0, 0)),
            ],
            out_specs=pl.BlockSpec(memory_space=pl.ANY),
            scratch_shapes=[pltpu.VMEM((d, f), BF16), pltpu.VMEM((d, f), BF16), pltpu.VMEM((f, d), BF16),
                            pltpu.VMEM((2, 2 * tm, d), F32), pltpu.VMEM((2, 2 * tm, d), F32),
                            pltpu.VMEM((tm, d), F32), pltpu.VMEM((tm, d), F32),
                            pltpu.SemaphoreType.DMA((2,)), pltpu.SemaphoreType.DMA((2,)), pltpu.SemaphoreType.DMA((2,))],
        ),
        out_shape=jax.ShapeDtypeStruct((n_rows, d), F32),
        compiler_params=_cparams(("arbitrary",)),
        name="expert_swiglu",
    )(block_first, block_count, xs, w_gate_up, b_gate.reshape(n_exp, 1, f), b_up.reshape(n_exp, 1, f),
      w_down, b_down.reshape(n_exp, 1, d), perm)


def _combine_kernel(dest_ref, ys_ref, gate_ref, x1_ref, g_ref, b_ref, pgw_ref, pgb_ref, p_ref, pw_ref,
                    out_ref, outb_ref, buf, sem, *, alpha):
    tm = x1_ref.shape[0]
    i = pl.program_id(0)
    slot = i % 2

    def gather(tile, into):
        base = tile * (tm * TOP_K)

        def issue(t, c):
            for k in range(TOP_K):
                d = dest_ref[base + t * TOP_K + k]
                pltpu.make_async_copy(ys_ref.at[pl.ds(d, 1), :], buf.at[into, k, pl.ds(t, 1), :], sem.at[into]).start()
            return c

        lax.fori_loop(0, tm, issue, 0)

    @pl.when(i == 0)
    def _():
        gather(0, 0)

    @pl.when(i + 1 < pl.num_programs(0))
    def _():
        gather(i + 1, 1 - slot)

    for k in range(TOP_K):
        pltpu.make_async_copy(ys_ref.at[pl.ds(0, tm), :], buf.at[slot, k], sem.at[slot]).wait()

    gates = gate_ref[...]
    ffn = gates[:, 0:1] * buf[slot, 0]
    for k in range(1, TOP_K):
        ffn = ffn + gates[:, k:k + 1] * buf[slot, k]
    x2 = _layer_norm(alpha * x1_ref[...] + ffn, g_ref[...], b_ref[...])
    ple_gate = jax.nn.sigmoid(jnp.dot(x2.astype(BF16), pgw_ref[...], preferred_element_type=F32) + pgb_ref[...])
    emb = jnp.dot(p_ref[...].astype(BF16), pw_ref[...], preferred_element_type=F32)
    out = x2 + ple_gate * emb
    out_ref[...] = out
    outb_ref[...] = out.astype(BF16)


def _combine(dest, ys, gates, x1, ln_g, ln_b, ple_gate_w, ple_gate_b, p, ple_w, alpha, tm=256):
    t, d = x1.shape
    dp = p.shape[1]
    tm = min(tm, t)
    tile = lambda i, dest: (i, 0)
    const = lambda i, dest: (0, 0)
    return pl.pallas_call(
        functools.partial(_combine_kernel, alpha=alpha),
        grid_spec=pltpu.PrefetchScalarGridSpec(
            num_scalar_prefetch=1,
            grid=(t // tm,),
            in_specs=[
                pl.BlockSpec(memory_space=pl.ANY),
                pl.BlockSpec((tm, LANES), tile),
                pl.BlockSpec((tm, d), tile),
                pl.BlockSpec((1, d), const),
                pl.BlockSpec((1, d), const),
                pl.BlockSpec((d, d), const),
                pl.BlockSpec((1, d), const),
                pl.BlockSpec((tm, dp), tile),
                pl.BlockSpec((dp, d), const),
            ],
            out_specs=[pl.BlockSpec((tm, d), tile), pl.BlockSpec((tm, d), tile)],
            scratch_shapes=[pltpu.VMEM((2, TOP_K, tm, d), F32), pltpu.SemaphoreType.DMA((2,))],
        ),
        out_shape=[jax.ShapeDtypeStruct((t, d), F32), jax.ShapeDtypeStruct((t, d), BF16)],
        compiler_params=_cparams(("arbitrary",)),
        name="combine_ln_ple",
    )(dest, ys, gates, x1, ln_g.reshape(1, d), ln_b.reshape(1, d), ple_gate_w, ple_gate_b.reshape(1, d), p, ple_w)


def _routing(idx_rank, counts, n_exp):
    top_idx = idx_rank[:, :TOP_K]
    rank = idx_rank[:, TOP_K:2 * TOP_K]
    counts = counts[0, :n_exp].astype(jnp.int32)
    padded = (counts + ROW_BLOCK - 1) // ROW_BLOCK * ROW_BLOCK
    pad_ends = jnp.cumsum(padded)
    pad_starts = pad_ends - padded
    experts = jnp.arange(n_exp, dtype=jnp.int32)
    start_of = jnp.sum(jnp.where(top_idx[:, :, None] == experts, pad_starts, 0), axis=-1)
    dest = (start_of + rank).reshape(-1).astype(jnp.int32)
    blocks = ((pad_starts // ROW_BLOCK).astype(jnp.int32), (padded // ROW_BLOCK).astype(jnp.int32))
    pad_rows = ((pad_starts + counts).astype(jnp.int32), pad_ends.astype(jnp.int32))
    return dest, blocks, pad_rows


def kernel(x, p, ret_w_in, ret_w_out, sb_w_in, sb_w_out, ln1_g, ln1_b, router_w, router_b, w_gate_up, b_gate_up,
           w_down, b_down, ln2_g, ln2_b, ple_w, ple_gate_w, ple_gate_b):
    batch, seq, d = x.shape
    depth = ln1_g.shape[0]
    n_exp = router_w.shape[-1]
    t = batch * seq
    alpha = float((2 * depth) ** 0.25)
    ret_heads = d // RET_QK_DIM
    n_blocks = -(-(t * TOP_K + n_exp * (ROW_BLOCK - 1)) // ROW_BLOCK)
    n_rows = n_blocks * ROW_BLOCK

    xf = x.reshape(t, d)
    xb = xf.astype(BF16)
    for i in range(depth):
        j = i // 2
        if i % 2 == 0:
            proj = _matmul(xb, ret_w_in[j].astype(BF16), BF16)
            mixed = _retention(proj, batch, seq, ret_heads)
            w_out = ret_w_out[j].astype(BF16)
        else:
            proj = _matmul(xb, sb_w_in[j].astype(BF16), BF16)
            mixed = _sb_attention(proj, batch, seq, d)
            w_out = sb_w_out[j].astype(BF16)
        x1, idx_rank, gates, counts = _mix_router(mixed, w_out, xf, ln1_g[i], ln1_b[i], router_w[i], router_b[i], alpha)
        dest, (block_first, block_count), (pad_lo, pad_hi) = _routing(idx_rank, counts, n_exp)
        xs = _dispatch(dest, pad_lo, pad_hi, x1, n_rows)
        ys = _experts(block_first, block_count, xs, i, w_gate_up, b_gate_up[i][:, 0::2], b_gate_up[i][:, 1::2],
                      w_down, b_down[i])
        xf, xb = _combine(dest, ys, gates, x1, ln2_g[i], ln2_b[i], ple_gate_w[i].astype(BF16), ple_gate_b[i],
                          p[i].reshape(t, -1), ple_w[i].astype(BF16), alpha)
    return xf.reshape(batch, seq, d)
```

```python
import functools

import jax
import jax.numpy as jnp
import numpy as np
from jax import lax
from jax.experimental import pallas as pl
from jax.experimental.pallas import tpu as pltpu

F32 = jnp.float32
BF16 = jnp.bfloat16

RET_QK_DIM = 256
RET_V_DIM = 512
ROPE_BASE = 10000.0
GN_EPS = 1e-6
LN_EPS = 1e-5
SB_HEADS = 16
TOP_K = 4
SWIGLU_LIMIT = 7.0
SWIGLU_ALPHA = 1.702

LANES = 128
MXU_DIM = 256
VMEM_LIMIT = 60 * 1024 * 1024

RET_BLOCK = 256
RET_CHUNK = 64
SB_TILE = 256
ROW_BLOCK = 256
PROJ_CHUNK = 1024
EXP_ZERO_BELOW = -104.0


def _cparams(sem):
    return pltpu.CompilerParams(dimension_semantics=sem, vmem_limit_bytes=VMEM_LIMIT)


def _mm_kernel(x_ref, w_ref, o_ref):
    o_ref[...] = jnp.dot(x_ref[...], w_ref[...], preferred_element_type=F32).astype(o_ref.dtype)


def _matmul(x, w, out_dtype, tm=1024, tn=1024):
    m, k = x.shape
    n = w.shape[1]
    tm = min(tm, m)
    tn = min(tn, n)
    return pl.pallas_call(
        _mm_kernel,
        grid=(n // tn, m // tm),
        in_specs=[pl.BlockSpec((tm, k), lambda j, i: (i, 0)),
                  pl.BlockSpec((k, tn), lambda j, i: (0, j))],
        out_specs=pl.BlockSpec((tm, tn), lambda j, i: (i, j)),
        out_shape=jax.ShapeDtypeStruct((m, n), out_dtype),
        compiler_params=_cparams(("parallel", "parallel")),
        name="proj_matmul",
    )(x, w)


def _retention_kernel(q_ref, k_ref, v_ref, g_ref, cos_ref, sin_ref, dmat_ref, qdec_ref, kdec_ref,
                      cdec_ref, o_ref, state_ref):
    half = RET_QK_DIM // 2
    cos = cos_ref[...]
    sin = sin_ref[...]

    def rot(t):
        t1, t2 = t[:, :half], t[:, half:]
        return jnp.concatenate([t1 * cos - t2 * sin, t1 * sin + t2 * cos], axis=-1)

    q = rot(q_ref[...].astype(F32))
    k = rot(k_ref[...].astype(F32)) * (RET_QK_DIM ** -0.5)
    v = v_ref[...]

    @pl.when(pl.program_id(2) == 0)
    def _():
        state_ref[...] = jnp.zeros_like(state_ref)

    state = state_ref[...]
    scores = lax.dot_general(q.astype(BF16), k.astype(BF16), (((1,), (1,)), ((), ())),
                             preferred_element_type=F32) * dmat_ref[...]
    inner = jnp.dot(scores.astype(BF16), v, preferred_element_type=F32)
    cross = jnp.dot((q * qdec_ref[...]).astype(BF16), state.astype(BF16), preferred_element_type=F32)
    kv = lax.dot_general((k * kdec_ref[...]).astype(BF16), v, (((0,), (0,)), ((), ())),
                         preferred_element_type=F32)
    state_ref[...] = state * cdec_ref[...] + kv

    out = inner + cross
    mu = jnp.mean(out, axis=-1, keepdims=True)
    var = jnp.mean(jnp.square(out - mu), axis=-1, keepdims=True)
    normed = (out - mu) * lax.rsqrt(var + GN_EPS)
    g = g_ref[...].astype(F32)
    o_ref[...] = (g * jax.nn.sigmoid(g) * normed).astype(o_ref.dtype)


def _retention_tables(seq, heads):
    half = RET_QK_DIM // 2
    inv_freq = 1.0 / (ROPE_BASE ** (jnp.arange(half, dtype=F32) / half))
    ang = jnp.arange(seq, dtype=F32)[:, None] * inv_freq[None, :]
    log_gamma = jnp.log(1.0 - 2.0 ** (-5.0 - jnp.arange(heads, dtype=F32)))
    pos = jnp.arange(RET_BLOCK)
    diff = (pos[:, None] - pos[None, :]).astype(F32)
    same_chunk = (pos[:, None] // RET_CHUNK) == (pos[None, :] // RET_CHUNK)
    earlier_chunk = (pos[None, :] // RET_CHUNK) < (pos[:, None] // RET_CHUNK)
    lg = log_gamma[:, None, None]
    dmat = jnp.where(same_chunk[None], jnp.exp(lg * jnp.abs(diff)[None]),
                     jnp.where(earlier_chunk[None], jnp.exp(lg * diff[None]), 0.0))
    idx = pos.astype(F32)
    qdec = jnp.exp(log_gamma[:, None] * (idx + 1.0))[:, :, None]
    kdec = jnp.exp(log_gamma[:, None] * (RET_BLOCK - 1.0 - idx))[:, :, None]
    cdec = jnp.exp(log_gamma * RET_BLOCK)[:, None, None]
    return jnp.cos(ang), jnp.sin(ang), dmat, qdec, kdec, cdec


def _retention(proj, batch, seq, heads):
    t = proj.shape[0]
    blk = RET_BLOCK
    nblk = seq // blk
    dk, dv = RET_QK_DIM, RET_V_DIM
    cos, sin, dmat, qdec, kdec, cdec = _retention_tables(seq, heads)
    row = lambda b, h, i: b * nblk + i
    return pl.pallas_call(
        _retention_kernel,
        grid=(batch, heads, nblk),
        in_specs=[
            pl.BlockSpec((blk, dk), lambda b, h, i: (row(b, h, i), h)),
            pl.BlockSpec((blk, dk), lambda b, h, i: (row(b, h, i), heads + h)),
            pl.BlockSpec((blk, dv), lambda b, h, i: (row(b, h, i), heads + h)),
            pl.BlockSpec((blk, dv), lambda b, h, i: (row(b, h, i), 2 * heads + h)),
            pl.BlockSpec((blk, dk // 2), lambda b, h, i: (i, 0)),
            pl.BlockSpec((blk, dk // 2), lambda b, h, i: (i, 0)),
            pl.BlockSpec((None, blk, blk), lambda b, h, i: (h, 0, 0)),
            pl.BlockSpec((None, blk, 1), lambda b, h, i: (h, 0, 0)),
            pl.BlockSpec((None, blk, 1), lambda b, h, i: (h, 0, 0)),
            pl.BlockSpec((None, 1, 1), lambda b, h, i: (h, 0, 0)),
        ],
        out_specs=pl.BlockSpec((blk, dv), lambda b, h, i: (row(b, h, i), h)),
        out_shape=jax.ShapeDtypeStruct((t, heads * dv), BF16),
        scratch_shapes=[pltpu.VMEM((dk, dv), F32)],
        compiler_params=_cparams(("parallel", "parallel", "arbitrary")),
        name="retention",
    )(proj, proj, proj, proj, cos, sin, dmat, qdec, kdec, cdec)


def _sb_kernel(q_ref, k_ref, v_ref, u_ref, o_ref, *, head_dim):
    tq = SB_TILE
    n_q = q_ref.shape[0] // tq
    lane = lax.broadcasted_iota(jnp.int32, (tq, LANES), 1)
    first = lane < head_dim
    scale = jnp.asarray(head_dim ** -0.5, BF16)
    u = u_ref[...]
    row = lax.broadcasted_iota(jnp.int32, (tq, tq), 0)
    col = lax.broadcasted_iota(jnp.int32, (tq, tq), 1)
    past = col < row

    def tile(q_heads, j, carry, diagonal):
        run0, run1, acc = carry
        start = pl.multiple_of(j * tq, tq)
        kt = k_ref[pl.ds(start, tq), :]
        vt = v_ref[pl.ds(start, tq), :]
        new_runs = []
        pvs = []
        for qh, run in zip(q_heads, (run0, run1)):
            z = lax.dot_general(qh, kt, (((1,), (1,)), ((), ())), preferred_element_type=F32)
            sp = jnp.maximum(z, 0.0) + jnp.log(1.0 + jnp.exp(-jnp.abs(z)))
            masked = jnp.where(past, sp, 0.0) if diagonal else sp
            hi = masked.astype(BF16)
            lo = (masked - hi.astype(F32)).astype(BF16)
            cs = jnp.dot(jnp.concatenate([hi, lo], axis=1), u, preferred_element_type=F32)
            a = jnp.exp(z - sp - cs[:, :tq] - jnp.concatenate([run, run], axis=1))
            if diagonal:
                a = jnp.where(past, a, 0.0)
            pvs.append(jnp.dot(a.astype(BF16), vt, preferred_element_type=F32))
            new_runs.append(run + cs[:, tq:])
        acc = acc + jnp.where(first, pvs[0], pvs[1])
        return new_runs[0], new_runs[1], acc

    def q_tile(i, carry):
        rows = pl.ds(pl.multiple_of(i * tq, tq), tq)
        q = q_ref[rows, :] * scale
        zero = jnp.zeros_like(q)
        q_heads = (jnp.where(first, q, zero), jnp.where(first, zero, q))
        zeros = jnp.zeros((tq, LANES), F32)
        run0, run1, acc = tile(q_heads, i, (zeros, zeros, zeros), True)

        def cond(c):
            return jnp.logical_and(c[0] >= 0, c[1] <= -EXP_ZERO_BELOW)

        def body(c):
            j, _, r0, r1, ac = c
            r0, r1, ac = tile(q_heads, j, (r0, r1, ac), False)
            return j - 1, jnp.min(jnp.minimum(r0, r1)), r0, r1, ac

        init = (i - 1, jnp.min(jnp.minimum(run0, run1)), run0, run1, acc)
        acc = lax.while_loop(cond, body, init)[4]
        o_ref[rows, :] = acc.astype(o_ref.dtype)
        return carry

    lax.fori_loop(0, n_q, q_tile, 0)


def _sb_attention(proj, batch, seq, d_model):
    t = proj.shape[0]
    head_dim = d_model // SB_HEADS
    assert 2 * head_dim == LANES
    groups = d_model // LANES
    tq = SB_TILE
    j = np.arange(tq)
    strictly_later = (j[:, None] > j[None, :]).astype(np.float32)
    u1 = np.concatenate([strictly_later, np.ones((tq, LANES), np.float32)], axis=1)
    u = jnp.asarray(np.concatenate([u1, u1], axis=0), BF16)
    return pl.pallas_call(
        functools.partial(_sb_kernel, head_dim=head_dim),
        grid=(batch, groups),
        in_specs=[
            pl.BlockSpec((seq, LANES), lambda b, p: (b, p)),
            pl.BlockSpec((seq, LANES), lambda b, p: (b, groups + p)),
            pl.BlockSpec((seq, LANES), lambda b, p: (b, 2 * groups + p)),
            pl.BlockSpec((2 * tq, tq + LANES), lambda b, p: (0, 0)),
        ],
        out_specs=pl.BlockSpec((seq, LANES), lambda b, p: (b, p)),
        out_shape=jax.ShapeDtypeStruct((t, d_model), BF16),
        compiler_params=_cparams(("parallel", "parallel")),
        name="stick_breaking",
    )(proj, proj, proj, u)


def _layer_norm(h, g, b):
    mu = jnp.mean(h, axis=-1, keepdims=True)
    var = jnp.mean(jnp.square(h - mu), axis=-1, keepdims=True)
    return (h - mu) * lax.rsqrt(var + LN_EPS) * g + b


def _split_bf16(x):
    hi = x.astype(BF16)
    return hi, (x - hi.astype(F32)).astype(BF16)


def _mix_router_kernel(a_ref, w_ref, x_ref, g_ref, b_ref, rw_ref, rb_ref, tri_ref,
                       x1_ref, idx_ref, gate_ref, cnt_ref, carry_ref, *, alpha):
    tm = a_ref.shape[0]

    @pl.when(pl.program_id(0) == 0)
    def _():
        carry_ref[...] = jnp.zeros_like(carry_ref)

    y = jnp.dot(a_ref[...], w_ref[...], preferred_element_type=F32)
    x1 = _layer_norm(alpha * x_ref[...] + y, g_ref[...], b_ref[...])
    x1_ref[...] = x1

    xh, xl = _split_bf16(x1)
    wh, wl = _split_bf16(rw_ref[...])
    logits = (jnp.dot(xh, wh, preferred_element_type=F32) + jnp.dot(xl, wh, preferred_element_type=F32)
              + jnp.dot(xh, wl, preferred_element_type=F32)) + rb_ref[...]

    lane = lax.broadcasted_iota(jnp.int32, (tm, LANES), 1)
    work = logits
    sel_idx, sel_val, onehots = [], [], []
    for _ in range(TOP_K):
        m = jnp.max(work, axis=-1, keepdims=True)
        sel = jnp.min(jnp.where(work == m, lane, LANES), axis=-1, keepdims=True)
        hit = lane == sel
        sel_idx.append(sel)
        sel_val.append(m)
        onehots.append(hit)
        work = jnp.where(hit, -jnp.inf, work)
    exps = [jnp.exp(v - sel_val[0]) for v in sel_val]
    denom = exps[0] + exps[1] + exps[2] + exps[3]

    member = jnp.zeros((tm, LANES), F32)
    for hit in onehots:
        member = member + jnp.where(hit, 1.0, 0.0)
    prefix = jnp.dot(tri_ref[...], member.astype(BF16), preferred_element_type=F32)
    base = carry_ref[0:1, :] + prefix
    idx_out = jnp.zeros((tm, LANES), jnp.int32)
    gate_out = jnp.zeros((tm, LANES), F32)
    for k in range(TOP_K):
        rank = jnp.sum(jnp.where(onehots[k], base, 0.0), axis=-1, keepdims=True).astype(jnp.int32)
        idx_out = jnp.where(lane == k, sel_idx[k], idx_out)
        idx_out = jnp.where(lane == TOP_K + k, rank, idx_out)
        gate_out = jnp.where(lane == k, exps[k] / denom, gate_out)
    idx_ref[...] = idx_out
    gate_ref[...] = gate_out
    carry_ref[...] = carry_ref[...] + jnp.sum(member, axis=0, keepdims=True)
    cnt_ref[...] = carry_ref[...]


def _mix_router(a, w_out, x, ln_g, ln_b, router_w, router_b, alpha, tm=512):
    t, kin = a.shape
    d = x.shape[1]
    e = router_w.shape[1]
    tm = min(tm, t)
    rw = jnp.pad(router_w, ((0, 0), (0, LANES - e)))
    rb = jnp.pad(router_b, (0, LANES - e), constant_values=-jnp.inf).reshape(1, LANES)
    r = np.arange(tm)
    tri = jnp.asarray((r[None, :] < r[:, None]).astype(np.float32), BF16)
    tile = lambda i: (i, 0)
    const = lambda i: (0, 0)
    return pl.pallas_call(
        functools.partial(_mix_router_kernel, alpha=alpha),
        grid=(t // tm,),
        in_specs=[
            pl.BlockSpec((tm, kin), tile),
            pl.BlockSpec((kin, d), const),
            pl.BlockSpec((tm, d), tile),
            pl.BlockSpec((1, d), const),
            pl.BlockSpec((1, d), const),
            pl.BlockSpec((d, LANES), const),
            pl.BlockSpec((1, LANES), const),
            pl.BlockSpec((tm, tm), const),
        ],
        out_specs=[
            pl.BlockSpec((tm, d), tile),
            pl.BlockSpec((tm, LANES), tile),
            pl.BlockSpec((tm, LANES), tile),
            pl.BlockSpec((8, LANES), const),
        ],
        out_shape=[
            jax.ShapeDtypeStruct((t, d), F32),
            jax.ShapeDtypeStruct((t, LANES), jnp.int32),
            jax.ShapeDtypeStruct((t, LANES), F32),
            jax.ShapeDtypeStruct((8, LANES), F32),
        ],
        scratch_shapes=[pltpu.VMEM((8, LANES), F32)],
        compiler_params=_cparams(("arbitrary",)),
        name="mix_ln_router",
    )(a, w_out, x, ln_g.reshape(1, d), ln_b.reshape(1, d), rw, rb, tri)


def _dispatch_kernel(dest_ref, pad_lo_ref, pad_hi_ref, x_ref, xs_ref, zrow, sem, zsem):
    tm = x_ref.shape[0]
    base = pl.program_id(0) * (tm * TOP_K)

    def row_copy(t, d):
        return pltpu.make_async_copy(x_ref.at[pl.ds(t, 1), :], xs_ref.at[pl.ds(d, 1), :], sem)

    def issue(t, c):
        for k in range(TOP_K):
            row_copy(t, dest_ref[base + t * TOP_K + k]).start()
        return c

    lax.fori_loop(0, tm, issue, 0)

    @pl.when(pl.program_id(0) == pl.num_programs(0) - 1)
    def _():
        zrow[...] = jnp.zeros_like(zrow)
        n_exp = pad_lo_ref.shape[0]
        blk = zrow.shape[0]
        first_free = pad_hi_ref[n_exp - 1] // blk
        n_blk = xs_ref.shape[0] // blk

        def zero_row(r):
            return pltpu.make_async_copy(zrow.at[pl.ds(0, 1), :], xs_ref.at[pl.ds(r, 1), :], zsem)

        def zero_block(j):
            return pltpu.make_async_copy(zrow, xs_ref.at[pl.ds(pl.multiple_of(j * blk, blk), blk), :], zsem)

        def start_rows(e, c):
            return lax.fori_loop(pad_lo_ref[e], pad_hi_ref[e], lambda r, c2: (zero_row(r).start(), c2)[1], c)

        def wait_rows(e, c):
            return lax.fori_loop(pad_lo_ref[e], pad_hi_ref[e], lambda r, c2: (zero_row(r).wait(), c2)[1], c)

        lax.fori_loop(0, n_exp, start_rows, 0)
        lax.fori_loop(first_free, n_blk, lambda j, c: (zero_block(j).start(), c)[1], 0)
        lax.fori_loop(0, n_exp, wait_rows, 0)
        lax.fori_loop(first_free, n_blk, lambda j, c: (zero_block(j).wait(), c)[1], 0)

    for k in range(TOP_K):
        pltpu.make_async_copy(x_ref, xs_ref.at[pl.ds(0, tm), :], sem).wait()


def _dispatch(dest, pad_lo, pad_hi, x1, n_rows, tm=512):
    t, d = x1.shape
    tm = min(tm, t)
    return pl.pallas_call(
        _dispatch_kernel,
        grid_spec=pltpu.PrefetchScalarGridSpec(
            num_scalar_prefetch=3,
            grid=(t // tm,),
            in_specs=[pl.BlockSpec((tm, d), lambda i, *_: (i, 0))],
            out_specs=pl.BlockSpec(memory_space=pl.ANY),
            scratch_shapes=[pltpu.VMEM((ROW_BLOCK, d), x1.dtype), pltpu.SemaphoreType.DMA, pltpu.SemaphoreType.DMA],
        ),
        out_shape=jax.ShapeDtypeStruct((n_rows, d), x1.dtype),
        compiler_params=_cparams(("arbitrary",)),
        name="dispatch_rows",
    )(dest, pad_lo, pad_hi, x1)


def _experts_kernel(bstart_ref, bcount_ref, xs_ref, wgu_ref, bg_ref, bu_ref, wd_ref, bd_ref, perm_ref,
                    ys_ref, wg_s, wu_s, wd_s, xbuf, ybuf, xtail, ytail, xsem, ysem, tsem, *, n_blocks):
    e = pl.program_id(0)
    first = bstart_ref[e]
    n = bcount_ref[e]
    tm = ROW_BLOCK
    big = 2 * tm
    f = wd_ref.shape[0]
    n_big = n // 2
    has_tail = n % 2 == 1

    def big_rows(j):
        return pl.ds(pl.multiple_of((first + 2 * j) * tm, tm), big)

    def x_copy(j, slot):
        return pltpu.make_async_copy(xs_ref.at[big_rows(j), :], xbuf.at[slot], xsem.at[slot])

    def y_copy(j, slot):
        return pltpu.make_async_copy(ybuf.at[slot], ys_ref.at[big_rows(j), :], ysem.at[slot])

    tail_rows = pl.ds(pl.multiple_of((first + n - 1) * tm, tm), tm)
    xt_copy = pltpu.make_async_copy(xs_ref.at[tail_rows, :], xtail, tsem.at[0])
    yt_copy = pltpu.make_async_copy(ytail, ys_ref.at[tail_rows, :], tsem.at[1])

    def ffn(x):
        x = x.astype(BF16)
        gate = jnp.dot(x, wg_s[...], preferred_element_type=F32) + bg_ref[...]
        up = jnp.dot(x, wu_s[...], preferred_element_type=F32) + bu_ref[...]
        gate = jnp.minimum(gate, SWIGLU_LIMIT)
        up = jnp.clip(up, -SWIGLU_LIMIT, SWIGLU_LIMIT)
        act = (up + 1.0) * (gate * jax.nn.sigmoid(gate * SWIGLU_ALPHA))
        return jnp.dot(act.astype(BF16), wd_s[...], preferred_element_type=F32) + bd_ref[...]

    @pl.when(n > 0)
    def _():
        @pl.when(n_big > 0)
        def _():
            x_copy(0, 0).start()

        @pl.when(has_tail)
        def _():
            xt_copy.start()

        half = MXU_DIM // 2
        for c in range(2 * f // MXU_DIM):
            blk = wgu_ref[:, c * MXU_DIM:(c + 1) * MXU_DIM].astype(BF16)
            sep = jnp.dot(blk, perm_ref[...], preferred_element_type=F32).astype(BF16)
            wg_s[:, c * half:(c + 1) * half] = sep[:, :half]
            wu_s[:, c * half:(c + 1) * half] = sep[:, half:]
        wd_s[...] = wd_ref[...].astype(BF16)

        def step(j, carry):
            slot = j % 2
            x_copy(j, slot).wait()

            @pl.when(j + 1 < n_big)
            def _():
                x_copy(j + 1, 1 - slot).start()

            @pl.when(j >= 2)
            def _():
                y_copy(j - 2, slot).wait()

            ybuf[slot] = ffn(xbuf[slot])
            y_copy(j, slot).start()
            return carry

        lax.fori_loop(0, n_big, step, 0)

        @pl.when(has_tail)
        def _():
            xt_copy.wait()
            ytail[...] = ffn(xtail[...])
            yt_copy.start()

        @pl.when(n_big >= 2)
        def _():
            y_copy(n_big - 2, n_big % 2).wait()

        @pl.when(n_big >= 1)
        def _():
            y_copy(n_big - 1, (n_big - 1) % 2).wait()

        @pl.when(has_tail)
        def _():
            yt_copy.wait()

    @pl.when(e == pl.num_programs(0) - 1)
    def _():
        used = first + n

        @pl.when(used < n_blocks)
        def _():
            ytail[...] = jnp.zeros_like(ytail)

            def fill(j, carry):
                cp = pltpu.make_async_copy(ytail, ys_ref.at[pl.ds(pl.multiple_of(j * tm, tm), tm), :], tsem.at[1])
                cp.start()
                cp.wait()
                return carry

            lax.fori_loop(used, n_blocks, fill, 0)


def _experts(block_first, block_count, xs, layer, w_gate_up, b_gate, b_up, w_down, b_down):
    n_rows, d = xs.shape
    _, n_exp, _, f2 = w_gate_up.shape
    f = f2 // 2
    tm = ROW_BLOCK
    c = np.arange(MXU_DIM)
    src = np.where(c < MXU_DIM // 2, 2 * c, 2 * (c - MXU_DIM // 2) + 1)
    perm = jnp.asarray((np.arange(MXU_DIM)[:, None] == src[None, :]).astype(np.float32), BF16)
    by_expert = lambda e, bf, bc: (e, 0, 0)
    by_layer_expert = lambda e, bf, bc: (layer, e, 0, 0)
    return pl.pallas_call(
        functools.partial(_experts_kernel, n_blocks=n_rows // tm),
        grid_spec=pltpu.PrefetchScalarGridSpec(
            num_scalar_prefetch=2,
            grid=(n_exp,),
            in_specs=[
                pl.BlockSpec(memory_space=pl.ANY),
                pl.BlockSpec((None, None, d, f2), by_layer_expert),
                pl.BlockSpec((None, 1, f), by_expert),
                pl.BlockSpec((None, 1, f), by_expert),
                pl.BlockSpec((None, None, f, d), by_layer_expert),
                pl.BlockSpec((None, 1, d), by_expert),
                pl.BlockSpec((MXU_DIM, MXU_DIM), lambda e, bf, bc: (0, 0)),
            ],
            out_specs=pl.BlockSpec(memory_space=pl.ANY),
            scratch_shapes=[pltpu.VMEM((d, f), BF16), pltpu.VMEM((d, f), BF16), pltpu.VMEM((f, d), BF16),
                            pltpu.VMEM((2, 2 * tm, d), F32), pltpu.VMEM((2, 2 * tm, d), F32),
                            pltpu.VMEM((tm, d), F32), pltpu.VMEM((tm, d), F32),
                            pltpu.SemaphoreType.DMA((2,)), pltpu.SemaphoreType.DMA((2,)), pltpu.SemaphoreType.DMA((2,))],
        ),
        out_shape=jax.ShapeDtypeStruct((n_rows, d), F32),
        compiler_params=_cparams(("arbitrary",)),
        name="expert_swiglu",
    )(block_first, block_count, xs, w_gate_up, b_gate.reshape(n_exp, 1, f), b_up.reshape(n_exp, 1, f),
      w_down, b_down.reshape(n_exp, 1, d), perm)


def _combine_kernel(dest_ref, ys_ref, gate_ref, x1_ref, g_ref, b_ref, pgw_ref, pgb_ref, p_ref, pw_ref, *rest,
                    alpha, has_next):
    if has_next:
        w_next_ref, out_ref, proj_ref, buf, sem = rest
    else:
        out_ref, buf, sem = rest
    tm = x1_ref.shape[0]
    i = pl.program_id(0)
    slot = i % 2

    def gather(tile, into):
        base = tile * (tm * TOP_K)

        def issue(t, c):
            for k in range(TOP_K):
                d = dest_ref[base + t * TOP_K + k]
                pltpu.make_async_copy(ys_ref.at[pl.ds(d, 1), :], buf.at[into, k, pl.ds(t, 1), :], sem.at[into]).start()
            return c

        lax.fori_loop(0, tm, issue, 0)

    @pl.when(i == 0)
    def _():
        gather(0, 0)

    @pl.when(i + 1 < pl.num_programs(0))
    def _():
        gather(i + 1, 1 - slot)

    for k in range(TOP_K):
        pltpu.make_async_copy(ys_ref.at[pl.ds(0, tm), :], buf.at[slot, k], sem.at[slot]).wait()

    gates = gate_ref[...]
    ffn = gates[:, 0:1] * buf[slot, 0]
    for k in range(1, TOP_K):
        ffn = ffn + gates[:, k:k + 1] * buf[slot, k]
    x2 = _layer_norm(alpha * x1_ref[...] + ffn, g_ref[...], b_ref[...])
    ple_gate = jax.nn.sigmoid(jnp.dot(x2.astype(BF16), pgw_ref[...], preferred_element_type=F32) + pgb_ref[...])
    emb = jnp.dot(p_ref[...].astype(BF16), pw_ref[...], preferred_element_type=F32)
    out = x2 + ple_gate * emb
    out_ref[...] = out
    if has_next:
        ob = out.astype(BF16)
        for c in range(0, proj_ref.shape[1], PROJ_CHUNK):
            proj_ref[:, c:c + PROJ_CHUNK] = jnp.dot(ob, w_next_ref[:, c:c + PROJ_CHUNK],
                                                    preferred_element_type=F32).astype(proj_ref.dtype)


def _combine(dest, ys, gates, x1, ln_g, ln_b, ple_gate_w, ple_gate_b, p, ple_w, w_next, alpha, tm=256):
    t, d = x1.shape
    dp = p.shape[1]
    tm = min(tm, t)
    tile = lambda i, dest: (i, 0)
    const = lambda i, dest: (0, 0)
    has_next = w_next is not None
    extra_in, extra_out_specs, extra_out_shapes = [], [], []
    if has_next:
        n = w_next.shape[1]
        assert n % PROJ_CHUNK == 0
        extra_in = [pl.BlockSpec((d, n), const, pipeline_mode=pl.Buffered(1))]
        extra_out_specs = [pl.BlockSpec((tm, n), tile)]
        extra_out_shapes = [jax.ShapeDtypeStruct((t, n), BF16)]
    outs = pl.pallas_call(
        functools.partial(_combine_kernel, alpha=alpha, has_next=has_next),
        grid_spec=pltpu.PrefetchScalarGridSpec(
            num_scalar_prefetch=1,
            grid=(t // tm,),
            in_specs=[
                pl.BlockSpec(memory_space=pl.ANY),
                pl.BlockSpec((tm, LANES), tile),
                pl.BlockSpec((tm, d), tile),
                pl.BlockSpec((1, d), const),
                pl.BlockSpec((1, d), const),
                pl.BlockSpec((d, d), const),
                pl.BlockSpec((1, d), const),
                pl.BlockSpec((tm, dp), tile),
                pl.BlockSpec((dp, d), const),
            ] + extra_in,
            out_specs=[pl.BlockSpec((tm, d), tile)] + extra_out_specs,
            scratch_shapes=[pltpu.VMEM((2, TOP_K, tm, d), F32), pltpu.SemaphoreType.DMA((2,))],
        ),
        out_shape=[jax.ShapeDtypeStruct((t, d), F32)] + extra_out_shapes,
        compiler_params=_cparams(("arbitrary",)),
        name="combine_ln_ple",
    )(dest, ys, gates, x1, ln_g.reshape(1, d), ln_b.reshape(1, d), ple_gate_w, ple_gate_b.reshape(1, d), p, ple_w,
      *([w_next] if has_next else []))
    return (outs[0], outs[1]) if has_next else (outs[0], None)


def _routing(idx_rank, counts, n_exp):
    top_idx = idx_rank[:, :TOP_K]
    rank = idx_rank[:, TOP_K:2 * TOP_K]
    counts = counts[0, :n_exp].astype(jnp.int32)
    padded = (counts + ROW_BLOCK - 1) // ROW_BLOCK * ROW_BLOCK
    pad_ends = jnp.cumsum(padded)
    pad_starts = pad_ends - padded
    experts = jnp.arange(n_exp, dtype=jnp.int32)
    start_of = jnp.sum(jnp.where(top_idx[:, :, None] == experts, pad_starts, 0), axis=-1)
    dest = (start_of + rank).reshape(-1).astype(jnp.int32)
    blocks = ((pad_starts // ROW_BLOCK).astype(jnp.int32), (padded // ROW_BLOCK).astype(jnp.int32))
    pad_rows = ((pad_starts + counts).astype(jnp.int32), pad_ends.astype(jnp.int32))
    return dest, blocks, pad_rows


def kernel(x, p, ret_w_in, ret_w_out, sb_w_in, sb_w_out, ln1_g, ln1_b, router_w, router_b, w_gate_up, b_gate_up,
           w_down, b_down, ln2_g, ln2_b, ple_w, ple_gate_w, ple_gate_b):
    batch, seq, d = x.shape
    depth = ln1_g.shape[0]
    n_exp = router_w.shape[-1]
    t = batch * seq
    alpha = float((2 * depth) ** 0.25)
    ret_heads = d // RET_QK_DIM
    n_blocks = -(-(t * TOP_K + n_exp * (ROW_BLOCK - 1)) // ROW_BLOCK)
    n_rows = n_blocks * ROW_BLOCK

    def w_in(layer):
        return (ret_w_in if layer % 2 == 0 else sb_w_in)[layer // 2].astype(BF16)

    xf = x.reshape(t, d)
    proj = _matmul(xf.astype(BF16), w_in(0), BF16)
    for i in range(depth):
        j = i // 2
        if i % 2 == 0:
            mixed = _retention(proj, batch, seq, ret_heads)
            w_out = ret_w_out[j].astype(BF16)
        else:
            mixed = _sb_attention(proj, batch, seq, d)
            w_out = sb_w_out[j].astype(BF16)
        x1, idx_rank, gates, counts = _mix_router(mixed, w_out, xf, ln1_g[i], ln1_b[i], router_w[i], router_b[i], alpha)
        dest, (block_first, block_count), (pad_lo, pad_hi) = _routing(idx_rank, counts, n_exp)
        xs = _dispatch(dest, pad_lo, pad_hi, x1, n_rows)
        ys = _experts(block_first, block_count, xs, i, w_gate_up, b_gate_up[i][:, 0::2], b_gate_up[i][:, 1::2],
                      w_down, b_down[i])
        xf, proj = _combine(dest, ys, gates, x1, ln2_g[i], ln2_b[i], ple_gate_w[i].astype(BF16), ple_gate_b[i],
                            p[i].reshape(t, -1), ple_w[i].astype(BF16), w_in(i + 1) if i + 1 < depth else None, alpha)
    return xf.reshape(batch, seq, d)
```

```python
import functools

import jax
import jax.numpy as jnp
import numpy as np
from jax import lax
from jax.experimental import pallas as pl
from jax.experimental.pallas import tpu as pltpu

F32 = jnp.float32
BF16 = jnp.bfloat16

RET_QK_DIM = 256
RET_V_DIM = 512
ROPE_BASE = 10000.0
GN_EPS = 1e-6
LN_EPS = 1e-5
SB_HEADS = 16
TOP_K = 4
SWIGLU_LIMIT = 7.0
SWIGLU_ALPHA = 1.702

LANES = 128
MXU_DIM = 256
VMEM_LIMIT = 60 * 1024 * 1024

RET_BLOCK = 256
RET_CHUNK = 64
SB_TILE = 256
ROW_BLOCK = 256
PROJ_CHUNK = 1024
GATHER_SLOTS = 3
EXP_ZERO_BELOW = -104.0


def _cparams(sem):
    return pltpu.CompilerParams(dimension_semantics=sem, vmem_limit_bytes=VMEM_LIMIT)


def _mm_kernel(x_ref, w_ref, o_ref):
    o_ref[...] = jnp.dot(x_ref[...], w_ref[...], preferred_element_type=F32).astype(o_ref.dtype)


def _matmul(x, w, out_dtype, tm=1024, tn=1024):
    m, k = x.shape
    n = w.shape[1]
    tm = min(tm, m)
    tn = min(tn, n)
    return pl.pallas_call(
        _mm_kernel,
        grid=(n // tn, m // tm),
        in_specs=[pl.BlockSpec((tm, k), lambda j, i: (i, 0)),
                  pl.BlockSpec((k, tn), lambda j, i: (0, j))],
        out_specs=pl.BlockSpec((tm, tn), lambda j, i: (i, j)),
        out_shape=jax.ShapeDtypeStruct((m, n), out_dtype),
        compiler_params=_cparams(("parallel", "parallel")),
        name="proj_matmul",
    )(x, w)


def _retention_kernel(q_ref, k_ref, v_ref, g_ref, cos_ref, sin_ref, dmat_ref, qdec_ref, kdec_ref,
                      cdec_ref, o_ref, state_ref):
    half = RET_QK_DIM // 2
    cos = cos_ref[...]
    sin = sin_ref[...]

    def rot(t):
        t1, t2 = t[:, :half], t[:, half:]
        return jnp.concatenate([t1 * cos - t2 * sin, t1 * sin + t2 * cos], axis=-1)

    q = rot(q_ref[...].astype(F32))
    k = rot(k_ref[...].astype(F32)) * (RET_QK_DIM ** -0.5)
    v = v_ref[...]

    @pl.when(pl.program_id(2) == 0)
    def _():
        state_ref[...] = jnp.zeros_like(state_ref)

    state = state_ref[...]
    scores = lax.dot_general(q.astype(BF16), k.astype(BF16), (((1,), (1,)), ((), ())),
                             preferred_element_type=F32) * dmat_ref[...]
    inner = jnp.dot(scores.astype(BF16), v, preferred_element_type=F32)
    cross = jnp.dot((q * qdec_ref[...]).astype(BF16), state.astype(BF16), preferred_element_type=F32)
    kv = lax.dot_general((k * kdec_ref[...]).astype(BF16), v, (((0,), (0,)), ((), ())),
                         preferred_element_type=F32)
    state_ref[...] = state * cdec_ref[...] + kv

    out = inner + cross
    mu = jnp.mean(out, axis=-1, keepdims=True)
    var = jnp.mean(jnp.square(out - mu), axis=-1, keepdims=True)
    normed = (out - mu) * lax.rsqrt(var + GN_EPS)
    g = g_ref[...].astype(F32)
    o_ref[...] = (g * jax.nn.sigmoid(g) * normed).astype(o_ref.dtype)


def _retention_tables(seq, heads):
    half = RET_QK_DIM // 2
    inv_freq = 1.0 / (ROPE_BASE ** (jnp.arange(half, dtype=F32) / half))
    ang = jnp.arange(seq, dtype=F32)[:, None] * inv_freq[None, :]
    log_gamma = jnp.log(1.0 - 2.0 ** (-5.0 - jnp.arange(heads, dtype=F32)))
    pos = jnp.arange(RET_BLOCK)
    diff = (pos[:, None] - pos[None, :]).astype(F32)
    same_chunk = (pos[:, None] // RET_CHUNK) == (pos[None, :] // RET_CHUNK)
    earlier_chunk = (pos[None, :] // RET_CHUNK) < (pos[:, None] // RET_CHUNK)
    lg = log_gamma[:, None, None]
    dmat = jnp.where(same_chunk[None], jnp.exp(lg * jnp.abs(diff)[None]),
                     jnp.where(earlier_chunk[None], jnp.exp(lg * diff[None]), 0.0))
    idx = pos.astype(F32)
    qdec = jnp.exp(log_gamma[:, None] * (idx + 1.0))[:, :, None]
    kdec = jnp.exp(log_gamma[:, None] * (RET_BLOCK - 1.0 - idx))[:, :, None]
    cdec = jnp.exp(log_gamma * RET_BLOCK)[:, None, None]
    return jnp.cos(ang), jnp.sin(ang), dmat, qdec, kdec, cdec


def _retention(proj, batch, seq, heads):
    t = proj.shape[0]
    blk = RET_BLOCK
    nblk = seq // blk
    dk, dv = RET_QK_DIM, RET_V_DIM
    cos, sin, dmat, qdec, kdec, cdec = _retention_tables(seq, heads)
    row = lambda b, h, i: b * nblk + i
    return pl.pallas_call(
        _retention_kernel,
        grid=(batch, heads, nblk),
        in_specs=[
            pl.BlockSpec((blk, dk), lambda b, h, i: (row(b, h, i), h)),
            pl.BlockSpec((blk, dk), lambda b, h, i: (row(b, h, i), heads + h)),
            pl.BlockSpec((blk, dv), lambda b, h, i: (row(b, h, i), heads + h)),
            pl.BlockSpec((blk, dv), lambda b, h, i: (row(b, h, i), 2 * heads + h)),
            pl.BlockSpec((blk, dk // 2), lambda b, h, i: (i, 0)),
            pl.BlockSpec((blk, dk // 2), lambda b, h, i: (i, 0)),
            pl.BlockSpec((None, blk, blk), lambda b, h, i: (h, 0, 0)),
            pl.BlockSpec((None, blk, 1), lambda b, h, i: (h, 0, 0)),
            pl.BlockSpec((None, blk, 1), lambda b, h, i: (h, 0, 0)),
            pl.BlockSpec((None, 1, 1), lambda b, h, i: (h, 0, 0)),
        ],
        out_specs=pl.BlockSpec((blk, dv), lambda b, h, i: (row(b, h, i), h)),
        out_shape=jax.ShapeDtypeStruct((t, heads * dv), BF16),
        scratch_shapes=[pltpu.VMEM((dk, dv), F32)],
        compiler_params=_cparams(("parallel", "parallel", "arbitrary")),
        name="retention",
    )(proj, proj, proj, proj, cos, sin, dmat, qdec, kdec, cdec)


def _sb_kernel(q_ref, k_ref, v_ref, u_ref, o_ref, *, head_dim):
    tq = SB_TILE
    n_q = q_ref.shape[0] // tq
    lane = lax.broadcasted_iota(jnp.int32, (tq, LANES), 1)
    first = lane < head_dim
    scale = jnp.asarray(head_dim ** -0.5, BF16)
    u = u_ref[...]
    row = lax.broadcasted_iota(jnp.int32, (tq, tq), 0)
    col = lax.broadcasted_iota(jnp.int32, (tq, tq), 1)
    past = col < row

    def tile(q_heads, j, carry, diagonal):
        run0, run1, acc = carry
        start = pl.multiple_of(j * tq, tq)
        kt = k_ref[pl.ds(start, tq), :]
        vt = v_ref[pl.ds(start, tq), :]
        new_runs = []
        pvs = []
        for qh, run in zip(q_heads, (run0, run1)):
            z = lax.dot_general(qh, kt, (((1,), (1,)), ((), ())), preferred_element_type=F32)
            sp = jnp.maximum(z, 0.0) + jnp.log(1.0 + jnp.exp(-jnp.abs(z)))
            masked = jnp.where(past, sp, 0.0) if diagonal else sp
            hi = masked.astype(BF16)
            lo = (masked - hi.astype(F32)).astype(BF16)
            cs = jnp.dot(jnp.concatenate([hi, lo], axis=1), u, preferred_element_type=F32)
            a = jnp.exp(z - sp - cs[:, :tq] - jnp.concatenate([run, run], axis=1))
            if diagonal:
                a = jnp.where(past, a, 0.0)
            pvs.append(jnp.dot(a.astype(BF16), vt, preferred_element_type=F32))
            new_runs.append(run + cs[:, tq:])
        acc = acc + jnp.where(first, pvs[0], pvs[1])
        return new_runs[0], new_runs[1], acc

    def q_tile(i, carry):
        rows = pl.ds(pl.multiple_of(i * tq, tq), tq)
        q = q_ref[rows, :] * scale
        zero = jnp.zeros_like(q)
        q_heads = (jnp.where(first, q, zero), jnp.where(first, zero, q))
        zeros = jnp.zeros((tq, LANES), F32)
        run0, run1, acc = tile(q_heads, i, (zeros, zeros, zeros), True)

        def cond(c):
            return jnp.logical_and(c[0] >= 0, c[1] <= -EXP_ZERO_BELOW)

        def body(c):
            j, _, r0, r1, ac = c
            r0, r1, ac = tile(q_heads, j, (r0, r1, ac), False)
            return j - 1, jnp.min(jnp.minimum(r0, r1)), r0, r1, ac

        init = (i - 1, jnp.min(jnp.minimum(run0, run1)), run0, run1, acc)
        acc = lax.while_loop(cond, body, init)[4]
        o_ref[rows, :] = acc.astype(o_ref.dtype)
        return carry

    lax.fori_loop(0, n_q, q_tile, 0)


def _sb_attention(proj, batch, seq, d_model):
    t = proj.shape[0]
    head_dim = d_model // SB_HEADS
    assert 2 * head_dim == LANES
    groups = d_model // LANES
    tq = SB_TILE
    j = np.arange(tq)
    strictly_later = (j[:, None] > j[None, :]).astype(np.float32)
    u1 = np.concatenate([strictly_later, np.ones((tq, LANES), np.float32)], axis=1)
    u = jnp.asarray(np.concatenate([u1, u1], axis=0), BF16)
    return pl.pallas_call(
        functools.partial(_sb_kernel, head_dim=head_dim),
        grid=(batch, groups),
        in_specs=[
            pl.BlockSpec((seq, LANES), lambda b, p: (b, p)),
            pl.BlockSpec((seq, LANES), lambda b, p: (b, groups + p)),
            pl.BlockSpec((seq, LANES), lambda b, p: (b, 2 * groups + p)),
            pl.BlockSpec((2 * tq, tq + LANES), lambda b, p: (0, 0)),
        ],
        out_specs=pl.BlockSpec((seq, LANES), lambda b, p: (b, p)),
        out_shape=jax.ShapeDtypeStruct((t, d_model), BF16),
        compiler_params=_cparams(("parallel", "parallel")),
        name="stick_breaking",
    )(proj, proj, proj, u)


def _layer_norm(h, g, b):
    mu = jnp.mean(h, axis=-1, keepdims=True)
    var = jnp.mean(jnp.square(h - mu), axis=-1, keepdims=True)
    return (h - mu) * lax.rsqrt(var + LN_EPS) * g + b


def _split_bf16(x):
    hi = x.astype(BF16)
    return hi, (x - hi.astype(F32)).astype(BF16)


def _mix_router_kernel(a_ref, w_ref, x_ref, g_ref, b_ref, rw_ref, rb_ref, tri_ref,
                       x1_ref, idx_ref, gate_ref, cnt_ref, carry_ref, *, alpha):
    tm = a_ref.shape[0]

    @pl.when(pl.program_id(0) == 0)
    def _():
        carry_ref[...] = jnp.zeros_like(carry_ref)

    y = jnp.dot(a_ref[...], w_ref[...], preferred_element_type=F32)
    x1 = _layer_norm(alpha * x_ref[...] + y, g_ref[...], b_ref[...])
    x1_ref[...] = x1

    xh, xl = _split_bf16(x1)
    wh, wl = _split_bf16(rw_ref[...])
    logits = (jnp.dot(xh, wh, preferred_element_type=F32) + jnp.dot(xl, wh, preferred_element_type=F32)
              + jnp.dot(xh, wl, preferred_element_type=F32)) + rb_ref[...]

    lane = lax.broadcasted_iota(jnp.int32, (tm, LANES), 1)
    work = logits
    sel_idx, sel_val, onehots = [], [], []
    for _ in range(TOP_K):
        m = jnp.max(work, axis=-1, keepdims=True)
        sel = jnp.min(jnp.where(work == m, lane, LANES), axis=-1, keepdims=True)
        hit = lane == sel
        sel_idx.append(sel)
        sel_val.append(m)
        onehots.append(hit)
        work = jnp.where(hit, -jnp.inf, work)
    exps = [jnp.exp(v - sel_val[0]) for v in sel_val]
    denom = exps[0] + exps[1] + exps[2] + exps[3]

    member = jnp.zeros((tm, LANES), F32)
    for hit in onehots:
        member = member + jnp.where(hit, 1.0, 0.0)
    prefix = jnp.dot(tri_ref[...], member.astype(BF16), preferred_element_type=F32)
    base = carry_ref[0:1, :] + prefix
    idx_out = jnp.zeros((tm, LANES), jnp.int32)
    gate_out = jnp.zeros((tm, LANES), F32)
    for k in range(TOP_K):
        rank = jnp.sum(jnp.where(onehots[k], base, 0.0), axis=-1, keepdims=True).astype(jnp.int32)
        idx_out = jnp.where(lane == k, sel_idx[k], idx_out)
        idx_out = jnp.where(lane == TOP_K + k, rank, idx_out)
        gate_out = jnp.where(lane == k, exps[k] / denom, gate_out)
    idx_ref[...] = idx_out
    gate_ref[...] = gate_out
    carry_ref[...] = carry_ref[...] + jnp.sum(member, axis=0, keepdims=True)
    cnt_ref[...] = carry_ref[...]


def _mix_router(a, w_out, x, ln_g, ln_b, router_w, router_b, alpha, tm=512):
    t, kin = a.shape
    d = x.shape[1]
    e = router_w.shape[1]
    tm = min(tm, t)
    rw = jnp.pad(router_w, ((0, 0), (0, LANES - e)))
    rb = jnp.pad(router_b, (0, LANES - e), constant_values=-jnp.inf).reshape(1, LANES)
    r = np.arange(tm)
    tri = jnp.asarray((r[None, :] < r[:, None]).astype(np.float32), BF16)
    tile = lambda i: (i, 0)
    const = lambda i: (0, 0)
    return pl.pallas_call(
        functools.partial(_mix_router_kernel, alpha=alpha),
        grid=(t // tm,),
        in_specs=[
            pl.BlockSpec((tm, kin), tile),
            pl.BlockSpec((kin, d), const),
            pl.BlockSpec((tm, d), tile),
            pl.BlockSpec((1, d), const),
            pl.BlockSpec((1, d), const),
            pl.BlockSpec((d, LANES), const),
            pl.BlockSpec((1, LANES), const),
            pl.BlockSpec((tm, tm), const),
        ],
        out_specs=[
            pl.BlockSpec((tm, d), tile),
            pl.BlockSpec((tm, LANES), tile),
            pl.BlockSpec((tm, LANES), tile),
            pl.BlockSpec((8, LANES), const),
        ],
        out_shape=[
            jax.ShapeDtypeStruct((t, d), F32),
            jax.ShapeDtypeStruct((t, LANES), jnp.int32),
            jax.ShapeDtypeStruct((t, LANES), F32),
            jax.ShapeDtypeStruct((8, LANES), F32),
        ],
        scratch_shapes=[pltpu.VMEM((8, LANES), F32)],
        compiler_params=_cparams(("arbitrary",)),
        name="mix_ln_router",
    )(a, w_out, x, ln_g.reshape(1, d), ln_b.reshape(1, d), rw, rb, tri)


def _dispatch_kernel(dest_ref, pad_lo_ref, pad_hi_ref, x_ref, xs_ref, zrow, sem, zsem):
    tm = x_ref.shape[0]
    base = pl.program_id(0) * (tm * TOP_K)

    def row_copy(t, d):
        return pltpu.make_async_copy(x_ref.at[pl.ds(t, 1), :], xs_ref.at[pl.ds(d, 1), :], sem)

    def issue(t, c):
        for k in range(TOP_K):
            row_copy(t, dest_ref[base + t * TOP_K + k]).start()
        return c

    lax.fori_loop(0, tm, issue, 0)

    @pl.when(pl.program_id(0) == pl.num_programs(0) - 1)
    def _():
        zrow[...] = jnp.zeros_like(zrow)
        n_exp = pad_lo_ref.shape[0]
        blk = zrow.shape[0]
        first_free = pad_hi_ref[n_exp - 1] // blk
        n_blk = xs_ref.shape[0] // blk

        def zero_row(r):
            return pltpu.make_async_copy(zrow.at[pl.ds(0, 1), :], xs_ref.at[pl.ds(r, 1), :], zsem)

        def zero_block(j):
            return pltpu.make_async_copy(zrow, xs_ref.at[pl.ds(pl.multiple_of(j * blk, blk), blk), :], zsem)

        def start_rows(e, c):
            return lax.fori_loop(pad_lo_ref[e], pad_hi_ref[e], lambda r, c2: (zero_row(r).start(), c2)[1], c)

        def wait_rows(e, c):
            return lax.fori_loop(pad_lo_ref[e], pad_hi_ref[e], lambda r, c2: (zero_row(r).wait(), c2)[1], c)

        lax.fori_loop(0, n_exp, start_rows, 0)
        lax.fori_loop(first_free, n_blk, lambda j, c: (zero_block(j).start(), c)[1], 0)
        lax.fori_loop(0, n_exp, wait_rows, 0)
        lax.fori_loop(first_free, n_blk, lambda j, c: (zero_block(j).wait(), c)[1], 0)

    for k in range(TOP_K):
        pltpu.make_async_copy(x_ref, xs_ref.at[pl.ds(0, tm), :], sem).wait()


def _dispatch(dest, pad_lo, pad_hi, x1, n_rows, tm=512):
    t, d = x1.shape
    tm = min(tm, t)
    return pl.pallas_call(
        _dispatch_kernel,
        grid_spec=pltpu.PrefetchScalarGridSpec(
            num_scalar_prefetch=3,
            grid=(t // tm,),
            in_specs=[pl.BlockSpec((tm, d), lambda i, *_: (i, 0))],
            out_specs=pl.BlockSpec(memory_space=pl.ANY),
            scratch_shapes=[pltpu.VMEM((ROW_BLOCK, d), x1.dtype), pltpu.SemaphoreType.DMA, pltpu.SemaphoreType.DMA],
        ),
        out_shape=jax.ShapeDtypeStruct((n_rows, d), x1.dtype),
        compiler_params=_cparams(("arbitrary",)),
        name="dispatch_rows",
    )(dest, pad_lo, pad_hi, x1)


def _experts_kernel(bstart_ref, bcount_ref, xs_ref, wgu_ref, bg_ref, bu_ref, wd_ref, bd_ref, perm_ref,
                    ys_ref, wg_s, wu_s, wd_s, xbuf, ybuf, xtail, ytail, xsem, ysem, tsem, *, n_blocks):
    e = pl.program_id(0)
    first = bstart_ref[e]
    n = bcount_ref[e]
    tm = ROW_BLOCK
    big = 2 * tm
    f = wd_ref.shape[0]
    n_big = n // 2
    has_tail = n % 2 == 1

    def big_rows(j):
        return pl.ds(pl.multiple_of((first + 2 * j) * tm, tm), big)

    def x_copy(j, slot):
        return pltpu.make_async_copy(xs_ref.at[big_rows(j), :], xbuf.at[slot], xsem.at[slot])

    def y_copy(j, slot):
        return pltpu.make_async_copy(ybuf.at[slot], ys_ref.at[big_rows(j), :], ysem.at[slot])

    tail_rows = pl.ds(pl.multiple_of((first + n - 1) * tm, tm), tm)
    xt_copy = pltpu.make_async_copy(xs_ref.at[tail_rows, :], xtail, tsem.at[0])
    yt_copy = pltpu.make_async_copy(ytail, ys_ref.at[tail_rows, :], tsem.at[1])

    def ffn(x):
        x = x.astype(BF16)
        gate = jnp.dot(x, wg_s[...], preferred_element_type=F32) + bg_ref[...]
        up = jnp.dot(x, wu_s[...], preferred_element_type=F32) + bu_ref[...]
        gate = jnp.minimum(gate, SWIGLU_LIMIT)
        up = jnp.clip(up, -SWIGLU_LIMIT, SWIGLU_LIMIT)
        act = (up + 1.0) * (gate * jax.nn.sigmoid(gate * SWIGLU_ALPHA))
        return jnp.dot(act.astype(BF16), wd_s[...], preferred_element_type=F32) + bd_ref[...]

    @pl.when(n > 0)
    def _():
        @pl.when(n_big > 0)
        def _():
            x_copy(0, 0).start()

        @pl.when(has_tail)
        def _():
            xt_copy.start()

        half = MXU_DIM // 2
        for c in range(2 * f // MXU_DIM):
            blk = wgu_ref[:, c * MXU_DIM:(c + 1) * MXU_DIM].astype(BF16)
            sep = jnp.dot(blk, perm_ref[...], preferred_element_type=F32).astype(BF16)
            wg_s[:, c * half:(c + 1) * half] = sep[:, :half]
            wu_s[:, c * half:(c + 1) * half] = sep[:, half:]
        wd_s[...] = wd_ref[...].astype(BF16)

        def step(j, carry):
            slot = j % 2
            x_copy(j, slot).wait()

            @pl.when(j + 1 < n_big)
            def _():
                x_copy(j + 1, 1 - slot).start()

            @pl.when(j >= 2)
            def _():
                y_copy(j - 2, slot).wait()

            ybuf[slot] = ffn(xbuf[slot])
            y_copy(j, slot).start()
            return carry

        lax.fori_loop(0, n_big, step, 0)

        @pl.when(has_tail)
        def _():
            xt_copy.wait()
            ytail[...] = ffn(xtail[...])
            yt_copy.start()

        @pl.when(n_big >= 2)
        def _():
            y_copy(n_big - 2, n_big % 2).wait()

        @pl.when(n_big >= 1)
        def _():
            y_copy(n_big - 1, (n_big - 1) % 2).wait()

        @pl.when(has_tail)
        def _():
            yt_copy.wait()

    @pl.when(e == pl.num_programs(0) - 1)
    def _():
        used = first + n

        @pl.when(used < n_blocks)
        def _():
            ytail[...] = jnp.zeros_like(ytail)

            def fill(j, carry):
                cp = pltpu.make_async_copy(ytail, ys_ref.at[pl.ds(pl.multiple_of(j * tm, tm), tm), :], tsem.at[1])
                cp.start()
                cp.wait()
                return carry

            lax.fori_loop(used, n_blocks, fill, 0)


def _experts(block_first, block_count, xs, layer, w_gate_up, b_gate, b_up, w_down, b_down):
    n_rows, d = xs.shape
    _, n_exp, _, f2 = w_gate_up.shape
    f = f2 // 2
    tm = ROW_BLOCK
    c = np.arange(MXU_DIM)
    src = np.where(c < MXU_DIM // 2, 2 * c, 2 * (c - MXU_DIM // 2) + 1)
    perm = jnp.asarray((np.arange(MXU_DIM)[:, None] == src[None, :]).astype(np.float32), BF16)
    by_expert = lambda e, bf, bc: (e, 0, 0)
    by_layer_expert = lambda e, bf, bc: (layer, e, 0, 0)
    return pl.pallas_call(
        functools.partial(_experts_kernel, n_blocks=n_rows // tm),
        grid_spec=pltpu.PrefetchScalarGridSpec(
            num_scalar_prefetch=2,
            grid=(n_exp,),
            in_specs=[
                pl.BlockSpec(memory_space=pl.ANY),
                pl.BlockSpec((None, None, d, f2), by_layer_expert),
                pl.BlockSpec((None, 1, f), by_expert),
                pl.BlockSpec((None, 1, f), by_expert),
                pl.BlockSpec((None, None, f, d), by_layer_expert),
                pl.BlockSpec((None, 1, d), by_expert),
                pl.BlockSpec((MXU_DIM, MXU_DIM), lambda e, bf, bc: (0, 0)),
            ],
            out_specs=pl.BlockSpec(memory_space=pl.ANY),
            scratch_shapes=[pltpu.VMEM((d, f), BF16), pltpu.VMEM((d, f), BF16), pltpu.VMEM((f, d), BF16),
                            pltpu.VMEM((2, 2 * tm, d), F32), pltpu.VMEM((2, 2 * tm, d), F32),
                            pltpu.VMEM((tm, d), F32), pltpu.VMEM((tm, d), F32),
                            pltpu.SemaphoreType.DMA((2,)), pltpu.SemaphoreType.DMA((2,)), pltpu.SemaphoreType.DMA((2,))],
        ),
        out_shape=jax.ShapeDtypeStruct((n_rows, d), F32),
        compiler_params=_cparams(("arbitrary",)),
        name="expert_swiglu",
    )(block_first, block_count, xs, w_gate_up, b_gate.reshape(n_exp, 1, f), b_up.reshape(n_exp, 1, f),
      w_down, b_down.reshape(n_exp, 1, d), perm)


def _combine_kernel(dest_ref, ys_ref, gate_ref, x1_ref, g_ref, b_ref, pgw_ref, pgb_ref, p_ref, pw_ref, *rest,
                    alpha, has_next, n_tiles):
    if has_next:
        w_next_ref, out_ref, proj_ref, buf, sem, obuf = rest
    else:
        out_ref, buf, sem = rest
    tm = x1_ref.shape[0]
    i = pl.program_id(0)
    slot = i % GATHER_SLOTS

    def issue_rows(tile, into, lo, hi):
        base = tile * (tm * TOP_K)

        def issue(t, c):
            for k in range(TOP_K):
                d = dest_ref[base + t * TOP_K + k]
                pltpu.make_async_copy(ys_ref.at[pl.ds(d, 1), :], buf.at[into, k, pl.ds(t, 1), :], sem.at[into]).start()
            return c

        lax.fori_loop(lo, hi, issue, 0)

    @pl.when(i == 0)
    def _():
        for first in range(min(GATHER_SLOTS - 1, n_tiles)):
            issue_rows(first, first, 0, tm)

    for k in range(TOP_K):
        pltpu.make_async_copy(ys_ref.at[pl.ds(0, tm), :], buf.at[slot, k], sem.at[slot]).wait()

    def reduce_and_norm():
        gates = gate_ref[...]
        ffn = gates[:, 0:1] * buf[slot, 0]
        for k in range(1, TOP_K):
            ffn = ffn + gates[:, k:k + 1] * buf[slot, k]
        out_ref[...] = _layer_norm(alpha * x1_ref[...] + ffn, g_ref[...], b_ref[...])

    def embed():
        x2 = out_ref[...]
        ple_gate = jax.nn.sigmoid(jnp.dot(x2.astype(BF16), pgw_ref[...], preferred_element_type=F32) + pgb_ref[...])
        emb = jnp.dot(p_ref[...].astype(BF16), pw_ref[...], preferred_element_type=F32)
        out = x2 + ple_gate * emb
        out_ref[...] = out
        if has_next:
            obuf[...] = out.astype(BF16)

    def project(c):
        def run():
            proj_ref[:, c:c + PROJ_CHUNK] = jnp.dot(obuf[...], w_next_ref[:, c:c + PROJ_CHUNK],
                                                    preferred_element_type=F32).astype(proj_ref.dtype)
        return run

    pieces = [reduce_and_norm, embed]
    if has_next:
        pieces += [project(c) for c in range(0, proj_ref.shape[1], PROJ_CHUNK)]

    ahead = i + (GATHER_SLOTS - 1)
    into = ahead % GATHER_SLOTS
    for j, piece in enumerate(pieces):
        @pl.when(ahead < n_tiles)
        def _():
            issue_rows(ahead, into, tm * j // len(pieces), tm * (j + 1) // len(pieces))

        piece()


def _combine(dest, ys, gates, x1, ln_g, ln_b, ple_gate_w, ple_gate_b, p, ple_w, w_next, alpha, tm=256):
    t, d = x1.shape
    dp = p.shape[1]
    tm = min(tm, t)
    tile = lambda i, dest: (i, 0)
    const = lambda i, dest: (0, 0)
    has_next = w_next is not None
    extra_in, extra_out_specs, extra_out_shapes = [], [], []
    if has_next:
        n = w_next.shape[1]
        assert n % PROJ_CHUNK == 0
        extra_in = [pl.BlockSpec((d, n), const, pipeline_mode=pl.Buffered(1))]
        extra_out_specs = [pl.BlockSpec((tm, n), tile)]
        extra_out_shapes = [jax.ShapeDtypeStruct((t, n), BF16)]
    outs = pl.pallas_call(
        functools.partial(_combine_kernel, alpha=alpha, has_next=has_next, n_tiles=t // tm),
        grid_spec=pltpu.PrefetchScalarGridSpec(
            num_scalar_prefetch=1,
            grid=(t // tm,),
            in_specs=[
                pl.BlockSpec(memory_space=pl.ANY),
                pl.BlockSpec((tm, LANES), tile),
                pl.BlockSpec((tm, d), tile),
                pl.BlockSpec((1, d), const),
                pl.BlockSpec((1, d), const),
                pl.BlockSpec((d, d), const),
                pl.BlockSpec((1, d), const),
                pl.BlockSpec((tm, dp), tile),
                pl.BlockSpec((dp, d), const),
            ] + extra_in,
            out_specs=[pl.BlockSpec((tm, d), tile)] + extra_out_specs,
            scratch_shapes=[pltpu.VMEM((GATHER_SLOTS, TOP_K, tm, d), F32), pltpu.SemaphoreType.DMA((GATHER_SLOTS,))]
            + ([pltpu.VMEM((tm, d), BF16)] if has_next else []),
        ),
        out_shape=[jax.ShapeDtypeStruct((t, d), F32)] + extra_out_shapes,
        compiler_params=_cparams(("arbitrary",)),
        name="combine_ln_ple",
    )(dest, ys, gates, x1, ln_g.reshape(1, d), ln_b.reshape(1, d), ple_gate_w, ple_gate_b.reshape(1, d), p, ple_w,
      *([w_next] if has_next else []))
    return (outs[0], outs[1]) if has_next else (outs[0], None)


def _routing(idx_rank, counts, n_exp):
    top_idx = idx_rank[:, :TOP_K]
    rank = idx_rank[:, TOP_K:2 * TOP_K]
    counts = counts[0, :n_exp].astype(jnp.int32)
    padded = (counts + ROW_BLOCK - 1) // ROW_BLOCK * ROW_BLOCK
    pad_ends = jnp.cumsum(padded)
    pad_starts = pad_ends - padded
    experts = jnp.arange(n_exp, dtype=jnp.int32)
    start_of = jnp.sum(jnp.where(top_idx[:, :, None] == experts, pad_starts, 0), axis=-1)
    dest = (start_of + rank).reshape(-1).astype(jnp.int32)
    blocks = ((pad_starts // ROW_BLOCK).astype(jnp.int32), (padded // ROW_BLOCK).astype(jnp.int32))
    pad_rows = ((pad_starts + counts).astype(jnp.int32), pad_ends.astype(jnp.int32))
    return dest, blocks, pad_rows


def kernel(x, p, ret_w_in, ret_w_out, sb_w_in, sb_w_out, ln1_g, ln1_b, router_w, router_b, w_gate_up, b_gate_up,
           w_down, b_down, ln2_g, ln2_b, ple_w, ple_gate_w, ple_gate_b):
    batch, seq, d = x.shape
    depth = ln1_g.shape[0]
    n_exp = router_w.shape[-1]
    t = batch * seq
    alpha = float((2 * depth) ** 0.25)
    ret_heads = d // RET_QK_DIM
    n_blocks = -(-(t * TOP_K + n_exp * (ROW_BLOCK - 1)) // ROW_BLOCK)
    n_rows = n_blocks * ROW_BLOCK

    def w_in(layer):
        return (ret_w_in if layer % 2 == 0 else sb_w_in)[layer // 2].astype(BF16)

    xf = x.reshape(t, d)
    proj = _matmul(xf.astype(BF16), w_in(0), BF16)
    for i in range(depth):
        j = i // 2
        if i % 2 == 0:
            mixed = _retention(proj, batch, seq, ret_heads)
            w_out = ret_w_out[j].astype(BF16)
        else:
            mixed = _sb_attention(proj, batch, seq, d)
            w_out = sb_w_out[j].astype(BF16)
        x1, idx_rank, gates, counts = _mix_router(mixed, w_out, xf, ln1_g[i], ln1_b[i], router_w[i], router_b[i], alpha)
        dest, (block_first, block_count), (pad_lo, pad_hi) = _routing(idx_rank, counts, n_exp)
        xs = _dispatch(dest, pad_lo, pad_hi, x1, n_rows)
        ys = _experts(block_first, block_count, xs, i, w_gate_up, b_gate_up[i][:, 0::2], b_gate_up[i][:, 1::2],
                      w_down, b_down[i])
        xf, proj = _combine(dest, ys, gates, x1, ln2_g[i], ln2_b[i], ple_gate_w[i].astype(BF16), ple_gate_b[i],
                            p[i].reshape(t, -1), ple_w[i].astype(BF16), w_in(i + 1) if i + 1 < depth else None, alpha)
    return xf.reshape(batch, seq, d)
```

```python
import functools

import jax
import jax.numpy as jnp
import numpy as np
from jax import lax
from jax.experimental import pallas as pl
from jax.experimental.pallas import tpu as pltpu

F32 = jnp.float32
BF16 = jnp.bfloat16

RET_QK_DIM = 256
RET_V_DIM = 512
ROPE_BASE = 10000.0
GN_EPS = 1e-6
LN_EPS = 1e-5
SB_HEADS = 16
TOP_K = 4
SWIGLU_LIMIT = 7.0
SWIGLU_ALPHA = 1.702

LANES = 128
MXU_DIM = 256
VMEM_LIMIT = 60 * 1024 * 1024

RET_BLOCK = 512
RET_CHUNK = 64
SB_TILE = 256
ROW_BLOCK = 256
PROJ_CHUNK = 1024
EXP_ZERO_BELOW = -104.0


def _cparams(sem):
    return pltpu.CompilerParams(dimension_semantics=sem, vmem_limit_bytes=VMEM_LIMIT)


def _mm_kernel(x_ref, w_ref, o_ref):
    o_ref[...] = jnp.dot(x_ref[...], w_ref[...], preferred_element_type=F32).astype(o_ref.dtype)


def _matmul(x, w, out_dtype, tm=1024, tn=1024):
    m, k = x.shape
    n = w.shape[1]
    tm = min(tm, m)
    tn = min(tn, n)
    return pl.pallas_call(
        _mm_kernel,
        grid=(n // tn, m // tm),
        in_specs=[pl.BlockSpec((tm, k), lambda j, i: (i, 0)),
                  pl.BlockSpec((k, tn), lambda j, i: (0, j))],
        out_specs=pl.BlockSpec((tm, tn), lambda j, i: (i, j)),
        out_shape=jax.ShapeDtypeStruct((m, n), out_dtype),
        compiler_params=_cparams(("parallel", "parallel")),
        name="proj_matmul",
    )(x, w)


def _retention_kernel(q_ref, k_ref, v_ref, g_ref, cos_ref, sin_ref, dmat_ref, qdec_ref, kdec_ref,
                      cdec_ref, o_ref, state_ref):
    half = RET_QK_DIM // 2
    cos = cos_ref[...]
    sin = sin_ref[...]

    def rot(t):
        t1, t2 = t[:, :half], t[:, half:]
        return jnp.concatenate([t1 * cos - t2 * sin, t1 * sin + t2 * cos], axis=-1)

    q = rot(q_ref[...].astype(F32))
    k = rot(k_ref[...].astype(F32)) * (RET_QK_DIM ** -0.5)
    v = v_ref[...]

    @pl.when(pl.program_id(2) == 0)
    def _():
        state_ref[...] = jnp.zeros_like(state_ref)

    state = state_ref[...]
    scores = lax.dot_general(q.astype(BF16), k.astype(BF16), (((1,), (1,)), ((), ())),
                             preferred_element_type=F32) * dmat_ref[...]
    inner = jnp.dot(scores.astype(BF16), v, preferred_element_type=F32)
    cross = jnp.dot((q * qdec_ref[...]).astype(BF16), state.astype(BF16), preferred_element_type=F32)
    kv = lax.dot_general((k * kdec_ref[...]).astype(BF16), v, (((0,), (0,)), ((), ())),
                         preferred_element_type=F32)
    state_ref[...] = state * cdec_ref[...] + kv

    out = inner + cross
    mu = jnp.mean(out, axis=-1, keepdims=True)
    var = jnp.mean(jnp.square(out - mu), axis=-1, keepdims=True)
    normed = (out - mu) * lax.rsqrt(var + GN_EPS)
    g = g_ref[...].astype(F32)
    o_ref[...] = (g * jax.nn.sigmoid(g) * normed).astype(o_ref.dtype)


def _retention_tables(seq, heads):
    half = RET_QK_DIM // 2
    inv_freq = 1.0 / (ROPE_BASE ** (jnp.arange(half, dtype=F32) / half))
    ang = jnp.arange(seq, dtype=F32)[:, None] * inv_freq[None, :]
    log_gamma = jnp.log(1.0 - 2.0 ** (-5.0 - jnp.arange(heads, dtype=F32)))
    pos = jnp.arange(RET_BLOCK)
    diff = (pos[:, None] - pos[None, :]).astype(F32)
    same_chunk = (pos[:, None] // RET_CHUNK) == (pos[None, :] // RET_CHUNK)
    earlier_chunk = (pos[None, :] // RET_CHUNK) < (pos[:, None] // RET_CHUNK)
    lg = log_gamma[:, None, None]
    dmat = jnp.where(same_chunk[None], jnp.exp(lg * jnp.abs(diff)[None]),
                     jnp.where(earlier_chunk[None], jnp.exp(lg * diff[None]), 0.0))
    idx = pos.astype(F32)
    qdec = jnp.exp(log_gamma[:, None] * (idx + 1.0))[:, :, None]
    kdec = jnp.exp(log_gamma[:, None] * (RET_BLOCK - 1.0 - idx))[:, :, None]
    cdec = jnp.exp(log_gamma * RET_BLOCK)[:, None, None]
    return jnp.cos(ang), jnp.sin(ang), dmat, qdec, kdec, cdec


def _retention(proj, batch, seq, heads):
    t = proj.shape[0]
    blk = RET_BLOCK
    nblk = seq // blk
    dk, dv = RET_QK_DIM, RET_V_DIM
    cos, sin, dmat, qdec, kdec, cdec = _retention_tables(seq, heads)
    row = lambda b, h, i: b * nblk + i
    return pl.pallas_call(
        _retention_kernel,
        grid=(batch, heads, nblk),
        in_specs=[
            pl.BlockSpec((blk, dk), lambda b, h, i: (row(b, h, i), h)),
            pl.BlockSpec((blk, dk), lambda b, h, i: (row(b, h, i), heads + h)),
            pl.BlockSpec((blk, dv), lambda b, h, i: (row(b, h, i), heads + h)),
            pl.BlockSpec((blk, dv), lambda b, h, i: (row(b, h, i), 2 * heads + h)),
            pl.BlockSpec((blk, dk // 2), lambda b, h, i: (i, 0)),
            pl.BlockSpec((blk, dk // 2), lambda b, h, i: (i, 0)),
            pl.BlockSpec((None, blk, blk), lambda b, h, i: (h, 0, 0)),
            pl.BlockSpec((None, blk, 1), lambda b, h, i: (h, 0, 0)),
            pl.BlockSpec((None, blk, 1), lambda b, h, i: (h, 0, 0)),
            pl.BlockSpec((None, 1, 1), lambda b, h, i: (h, 0, 0)),
        ],
        out_specs=pl.BlockSpec((blk, dv), lambda b, h, i: (row(b, h, i), h)),
        out_shape=jax.ShapeDtypeStruct((t, heads * dv), BF16),
        scratch_shapes=[pltpu.VMEM((dk, dv), F32)],
        compiler_params=_cparams(("parallel", "parallel", "arbitrary")),
        name="retention",
    )(proj, proj, proj, proj, cos, sin, dmat, qdec, kdec, cdec)


def _sb_kernel(q_ref, k_ref, v_ref, u_ref, o_ref, *, head_dim):
    tq = SB_TILE
    n_q = q_ref.shape[0] // tq
    lane = lax.broadcasted_iota(jnp.int32, (tq, LANES), 1)
    first = lane < head_dim
    scale = jnp.asarray(head_dim ** -0.5, BF16)
    u = u_ref[...]
    row = lax.broadcasted_iota(jnp.int32, (tq, tq), 0)
    col = lax.broadcasted_iota(jnp.int32, (tq, tq), 1)
    past = col < row

    def tile(q_heads, j, carry, diagonal):
        run0, run1, acc = carry
        start = pl.multiple_of(j * tq, tq)
        kt = k_ref[pl.ds(start, tq), :]
        vt = v_ref[pl.ds(start, tq), :]
        new_runs = []
        pvs = []
        for qh, run in zip(q_heads, (run0, run1)):
            z = lax.dot_general(qh, kt, (((1,), (1,)), ((), ())), preferred_element_type=F32)
            sp = jnp.maximum(z, 0.0) + jnp.log(1.0 + jnp.exp(-jnp.abs(z)))
            masked = jnp.where(past, sp, 0.0) if diagonal else sp
            hi = masked.astype(BF16)
            lo = (masked - hi.astype(F32)).astype(BF16)
            cs = jnp.dot(jnp.concatenate([hi, lo], axis=1), u, preferred_element_type=F32)
            a = jnp.exp(z - sp - cs[:, :tq] - jnp.concatenate([run, run], axis=1))
            if diagonal:
                a = jnp.where(past, a, 0.0)
            pvs.append(jnp.dot(a.astype(BF16), vt, preferred_element_type=F32))
            new_runs.append(run + cs[:, tq:])
        acc = acc + jnp.where(first, pvs[0], pvs[1])
        return new_runs[0], new_runs[1], acc

    def q_tile(i, carry):
        rows = pl.ds(pl.multiple_of(i * tq, tq), tq)
        q = q_ref[rows, :] * scale
        zero = jnp.zeros_like(q)
        q_heads = (jnp.where(first, q, zero), jnp.where(first, zero, q))
        zeros = jnp.zeros((tq, LANES), F32)
        run0, run1, acc = tile(q_heads, i, (zeros, zeros, zeros), True)

        def cond(c):
            return jnp.logical_and(c[0] >= 0, c[1] <= -EXP_ZERO_BELOW)

        def body(c):
            j, _, r0, r1, ac = c
            r0, r1, ac = tile(q_heads, j, (r0, r1, ac), False)
            return j - 1, jnp.min(jnp.minimum(r0, r1)), r0, r1, ac

        init = (i - 1, jnp.min(jnp.minimum(run0, run1)), run0, run1, acc)
        acc = lax.while_loop(cond, body, init)[4]
        o_ref[rows, :] = acc.astype(o_ref.dtype)
        return carry

    lax.fori_loop(0, n_q, q_tile, 0)


def _sb_attention(proj, batch, seq, d_model):
    t = proj.shape[0]
    head_dim = d_model // SB_HEADS
    assert 2 * head_dim == LANES
    groups = d_model // LANES
    tq = SB_TILE
    j = np.arange(tq)
    strictly_later = (j[:, None] > j[None, :]).astype(np.float32)
    u1 = np.concatenate([strictly_later, np.ones((tq, LANES), np.float32)], axis=1)
    u = jnp.asarray(np.concatenate([u1, u1], axis=0), BF16)
    return pl.pallas_call(
        functools.partial(_sb_kernel, head_dim=head_dim),
        grid=(batch, groups),
        in_specs=[
            pl.BlockSpec((seq, LANES), lambda b, p: (b, p)),
            pl.BlockSpec((seq, LANES), lambda b, p: (b, groups + p)),
            pl.BlockSpec((seq, LANES), lambda b, p: (b, 2 * groups + p)),
            pl.BlockSpec((2 * tq, tq + LANES), lambda b, p: (0, 0)),
        ],
        out_specs=pl.BlockSpec((seq, LANES), lambda b, p: (b, p)),
        out_shape=jax.ShapeDtypeStruct((t, d_model), BF16),
        compiler_params=_cparams(("parallel", "parallel")),
        name="stick_breaking",
    )(proj, proj, proj, u)


def _layer_norm(h, g, b):
    mu = jnp.mean(h, axis=-1, keepdims=True)
    var = jnp.mean(jnp.square(h - mu), axis=-1, keepdims=True)
    return (h - mu) * lax.rsqrt(var + LN_EPS) * g + b


def _split_bf16(x):
    hi = x.astype(BF16)
    return hi, (x - hi.astype(F32)).astype(BF16)


def _mix_router_kernel(a_ref, w_ref, x_ref, g_ref, b_ref, rw_ref, rb_ref, tri_ref,
                       x1_ref, idx_ref, gate_ref, cnt_ref, carry_ref, *, alpha):
    tm = a_ref.shape[0]

    @pl.when(pl.program_id(0) == 0)
    def _():
        carry_ref[...] = jnp.zeros_like(carry_ref)

    y = jnp.dot(a_ref[...], w_ref[...], preferred_element_type=F32)
    x1 = _layer_norm(alpha * x_ref[...] + y, g_ref[...], b_ref[...])
    x1_ref[...] = x1

    xh, xl = _split_bf16(x1)
    wh, wl = _split_bf16(rw_ref[...])
    logits = (jnp.dot(xh, wh, preferred_element_type=F32) + jnp.dot(xl, wh, preferred_element_type=F32)
              + jnp.dot(xh, wl, preferred_element_type=F32)) + rb_ref[...]

    lane = lax.broadcasted_iota(jnp.int32, (tm, LANES), 1)
    work = logits
    sel_idx, sel_val, onehots = [], [], []
    for _ in range(TOP_K):
        m = jnp.max(work, axis=-1, keepdims=True)
        sel = jnp.min(jnp.where(work == m, lane, LANES), axis=-1, keepdims=True)
        hit = lane == sel
        sel_idx.append(sel)
        sel_val.append(m)
        onehots.append(hit)
        work = jnp.where(hit, -jnp.inf, work)
    exps = [jnp.exp(v - sel_val[0]) for v in sel_val]
    denom = exps[0] + exps[1] + exps[2] + exps[3]

    member = jnp.zeros((tm, LANES), F32)
    for hit in onehots:
        member = member + jnp.where(hit, 1.0, 0.0)
    prefix = jnp.dot(tri_ref[...], member.astype(BF16), preferred_element_type=F32)
    base = carry_ref[0:1, :] + prefix
    idx_out = jnp.zeros((tm, LANES), jnp.int32)
    gate_out = jnp.zeros((tm, LANES), F32)
    for k in range(TOP_K):
        rank = jnp.sum(jnp.where(onehots[k], base, 0.0), axis=-1, keepdims=True).astype(jnp.int32)
        idx_out = jnp.where(lane == k, sel_idx[k], idx_out)
        idx_out = jnp.where(lane == TOP_K + k, rank, idx_out)
        gate_out = jnp.where(lane == k, exps[k] / denom, gate_out)
    idx_ref[...] = idx_out
    gate_ref[...] = gate_out
    carry_ref[...] = carry_ref[...] + jnp.sum(member, axis=0, keepdims=True)
    cnt_ref[...] = carry_ref[...]


def _mix_router(a, w_out, x, ln_g, ln_b, router_w, router_b, alpha, tm=512):
    t, kin = a.shape
    d = x.shape[1]
    e = router_w.shape[1]
    tm = min(tm, t)
    rw = jnp.pad(router_w, ((0, 0), (0, LANES - e)))
    rb = jnp.pad(router_b, (0, LANES - e), constant_values=-jnp.inf).reshape(1, LANES)
    r = np.arange(tm)
    tri = jnp.asarray((r[None, :] < r[:, None]).astype(np.float32), BF16)
    tile = lambda i: (i, 0)
    const = lambda i: (0, 0)
    return pl.pallas_call(
        functools.partial(_mix_router_kernel, alpha=alpha),
        grid=(t // tm,),
        in_specs=[
            pl.BlockSpec((tm, kin), tile),
            pl.BlockSpec((kin, d), const),
            pl.BlockSpec((tm, d), tile),
            pl.BlockSpec((1, d), const),
            pl.BlockSpec((1, d), const),
            pl.BlockSpec((d, LANES), const),
            pl.BlockSpec((1, LANES), const),
            pl.BlockSpec((tm, tm), const),
        ],
        out_specs=[
            pl.BlockSpec((tm, d), tile),
            pl.BlockSpec((tm, LANES), tile),
            pl.BlockSpec((tm, LANES), tile),
            pl.BlockSpec((8, LANES), const),
        ],
        out_shape=[
            jax.ShapeDtypeStruct((t, d), F32),
            jax.ShapeDtypeStruct((t, LANES), jnp.int32),
            jax.ShapeDtypeStruct((t, LANES), F32),
            jax.ShapeDtypeStruct((8, LANES), F32),
        ],
        scratch_shapes=[pltpu.VMEM((8, LANES), F32)],
        compiler_params=_cparams(("arbitrary",)),
        name="mix_ln_router",
    )(a, w_out, x, ln_g.reshape(1, d), ln_b.reshape(1, d), rw, rb, tri)


def _dispatch_kernel(dest_ref, pad_lo_ref, pad_hi_ref, x_ref, xs_ref, zrow, sem, zsem):
    tm = x_ref.shape[0]
    base = pl.program_id(0) * (tm * TOP_K)

    def row_copy(t, d):
        return pltpu.make_async_copy(x_ref.at[pl.ds(t, 1), :], xs_ref.at[pl.ds(d, 1), :], sem)

    def issue(t, c):
        for k in range(TOP_K):
            row_copy(t, dest_ref[base + t * TOP_K + k]).start()
        return c

    lax.fori_loop(0, tm, issue, 0)

    @pl.when(pl.program_id(0) == pl.num_programs(0) - 1)
    def _():
        zrow[...] = jnp.zeros_like(zrow)
        n_exp = pad_lo_ref.shape[0]
        blk = zrow.shape[0]
        first_free = pad_hi_ref[n_exp - 1] // blk
        n_blk = xs_ref.shape[0] // blk

        def zero_row(r):
            return pltpu.make_async_copy(zrow.at[pl.ds(0, 1), :], xs_ref.at[pl.ds(r, 1), :], zsem)

        def zero_block(j):
            return pltpu.make_async_copy(zrow, xs_ref.at[pl.ds(pl.multiple_of(j * blk, blk), blk), :], zsem)

        def start_rows(e, c):
            return lax.fori_loop(pad_lo_ref[e], pad_hi_ref[e], lambda r, c2: (zero_row(r).start(), c2)[1], c)

        def wait_rows(e, c):
            return lax.fori_loop(pad_lo_ref[e], pad_hi_ref[e], lambda r, c2: (zero_row(r).wait(), c2)[1], c)

        lax.fori_loop(0, n_exp, start_rows, 0)
        lax.fori_loop(first_free, n_blk, lambda j, c: (zero_block(j).start(), c)[1], 0)
        lax.fori_loop(0, n_exp, wait_rows, 0)
        lax.fori_loop(first_free, n_blk, lambda j, c: (zero_block(j).wait(), c)[1], 0)

    for k in range(TOP_K):
        pltpu.make_async_copy(x_ref, xs_ref.at[pl.ds(0, tm), :], sem).wait()


def _dispatch(dest, pad_lo, pad_hi, x1, n_rows, tm=512):
    t, d = x1.shape
    tm = min(tm, t)
    return pl.pallas_call(
        _dispatch_kernel,
        grid_spec=pltpu.PrefetchScalarGridSpec(
            num_scalar_prefetch=3,
            grid=(t // tm,),
            in_specs=[pl.BlockSpec((tm, d), lambda i, *_: (i, 0))],
            out_specs=pl.BlockSpec(memory_space=pl.ANY),
            scratch_shapes=[pltpu.VMEM((ROW_BLOCK, d), x1.dtype), pltpu.SemaphoreType.DMA, pltpu.SemaphoreType.DMA],
        ),
        out_shape=jax.ShapeDtypeStruct((n_rows, d), x1.dtype),
        compiler_params=_cparams(("arbitrary",)),
        name="dispatch_rows",
    )(dest, pad_lo, pad_hi, x1)


def _experts_kernel(bstart_ref, bcount_ref, xs_ref, wgu_ref, bg_ref, bu_ref, wd_ref, bd_ref, perm_ref,
                    ys_ref, wg_s, wu_s, wd_s, xbuf, ybuf, xtail, ytail, xsem, ysem, tsem, *, n_blocks):
    e = pl.program_id(0)
    first = bstart_ref[e]
    n = bcount_ref[e]
    tm = ROW_BLOCK
    big = 2 * tm
    f = wd_ref.shape[0]
    n_big = n // 2
    has_tail = n % 2 == 1

    def big_rows(j):
        return pl.ds(pl.multiple_of((first + 2 * j) * tm, tm), big)

    def x_copy(j, slot):
        return pltpu.make_async_copy(xs_ref.at[big_rows(j), :], xbuf.at[slot], xsem.at[slot])

    def y_copy(j, slot):
        return pltpu.make_async_copy(ybuf.at[slot], ys_ref.at[big_rows(j), :], ysem.at[slot])

    tail_rows = pl.ds(pl.multiple_of((first + n - 1) * tm, tm), tm)
    xt_copy = pltpu.make_async_copy(xs_ref.at[tail_rows, :], xtail, tsem.at[0])
    yt_copy = pltpu.make_async_copy(ytail, ys_ref.at[tail_rows, :], tsem.at[1])

    def ffn(x):
        x = x.astype(BF16)
        gate = jnp.dot(x, wg_s[...], preferred_element_type=F32) + bg_ref[...]
        up = jnp.dot(x, wu_s[...], preferred_element_type=F32) + bu_ref[...]
        gate = jnp.minimum(gate, SWIGLU_LIMIT)
        up = jnp.clip(up, -SWIGLU_LIMIT, SWIGLU_LIMIT)
        act = (up + 1.0) * (gate * jax.nn.sigmoid(gate * SWIGLU_ALPHA))
        return jnp.dot(act.astype(BF16), wd_s[...], preferred_element_type=F32) + bd_ref[...]

    @pl.when(n > 0)
    def _():
        @pl.when(n_big > 0)
        def _():
            x_copy(0, 0).start()

        @pl.when(has_tail)
        def _():
            xt_copy.start()

        half = MXU_DIM // 2
        for c in range(2 * f // MXU_DIM):
            blk = wgu_ref[:, c * MXU_DIM:(c + 1) * MXU_DIM].astype(BF16)
            sep = jnp.dot(blk, perm_ref[...], preferred_element_type=F32).astype(BF16)
            wg_s[:, c * half:(c + 1) * half] = sep[:, :half]
            wu_s[:, c * half:(c + 1) * half] = sep[:, half:]
        wd_s[...] = wd_ref[...].astype(BF16)

        def step(j, carry):
            slot = j % 2
            x_copy(j, slot).wait()

            @pl.when(j + 1 < n_big)
            def _():
                x_copy(j + 1, 1 - slot).start()

            @pl.when(j >= 2)
            def _():
                y_copy(j - 2, slot).wait()

            ybuf[slot] = ffn(xbuf[slot])
            y_copy(j, slot).start()
            return carry

        lax.fori_loop(0, n_big, step, 0)

        @pl.when(has_tail)
        def _():
            xt_copy.wait()
            ytail[...] = ffn(xtail[...])
            yt_copy.start()

        @pl.when(n_big >= 2)
        def _():
            y_copy(n_big - 2, n_big % 2).wait()

        @pl.when(n_big >= 1)
        def _():
            y_copy(n_big - 1, (n_big - 1) % 2).wait()

        @pl.when(has_tail)
        def _():
            yt_copy.wait()

    @pl.when(e == pl.num_programs(0) - 1)
    def _():
        used = first + n

        @pl.when(used < n_blocks)
        def _():
            ytail[...] = jnp.zeros_like(ytail)

            def fill(j, carry):
                cp = pltpu.make_async_copy(ytail, ys_ref.at[pl.ds(pl.multiple_of(j * tm, tm), tm), :], tsem.at[1])
                cp.start()
                cp.wait()
                return carry

            lax.fori_loop(used, n_blocks, fill, 0)


def _experts(block_first, block_count, xs, layer, w_gate_up, b_gate, b_up, w_down, b_down):
    n_rows, d = xs.shape
    _, n_exp, _, f2 = w_gate_up.shape
    f = f2 // 2
    tm = ROW_BLOCK
    c = np.arange(MXU_DIM)
    src = np.where(c < MXU_DIM // 2, 2 * c, 2 * (c - MXU_DIM // 2) + 1)
    perm = jnp.asarray((np.arange(MXU_DIM)[:, None] == src[None, :]).astype(np.float32), BF16)
    by_expert = lambda e, bf, bc: (e, 0, 0)
    by_layer_expert = lambda e, bf, bc: (layer, e, 0, 0)
    return pl.pallas_call(
        functools.partial(_experts_kernel, n_blocks=n_rows // tm),
        grid_spec=pltpu.PrefetchScalarGridSpec(
            num_scalar_prefetch=2,
            grid=(n_exp,),
            in_specs=[
                pl.BlockSpec(memory_space=pl.ANY),
                pl.BlockSpec((None, None, d, f2), by_layer_expert),
                pl.BlockSpec((None, 1, f), by_expert),
                pl.BlockSpec((None, 1, f), by_expert),
                pl.BlockSpec((None, None, f, d), by_layer_expert),
                pl.BlockSpec((None, 1, d), by_expert),
                pl.BlockSpec((MXU_DIM, MXU_DIM), lambda e, bf, bc: (0, 0)),
            ],
            out_specs=pl.BlockSpec(memory_space=pl.ANY),
            scratch_shapes=[pltpu.VMEM((d, f), BF16), pltpu.VMEM((d, f), BF16), pltpu.VMEM((f, d), BF16),
                            pltpu.VMEM((2, 2 * tm, d), F32), pltpu.VMEM((2, 2 * tm, d), F32),
                            pltpu.VMEM((tm, d), F32), pltpu.VMEM((tm, d), F32),
                            pltpu.SemaphoreType.DMA((2,)), pltpu.SemaphoreType.DMA((2,)), pltpu.SemaphoreType.DMA((2,))],
        ),
        out_shape=jax.ShapeDtypeStruct((n_rows, d), F32),
        compiler_params=_cparams(("arbitrary",)),
        name="expert_swiglu",
    )(block_first, block_count, xs, w_gate_up, b_gate.reshape(n_exp, 1, f), b_up.reshape(n_exp, 1, f),
      w_down, b_down.reshape(n_exp, 1, d), perm)


def _combine_kernel(dest_ref, ys_ref, gate_ref, x1_ref, g_ref, b_ref, pgw_ref, pgb_ref, p_ref, pw_ref, *rest,
                    alpha, has_next):
    if has_next:
        w_next_ref, out_ref, proj_ref, buf, sem = rest
    else:
        out_ref, buf, sem = rest
    tm = x1_ref.shape[0]
    i = pl.program_id(0)
    slot = i % 2

    def gather(tile, into):
        base = tile * (tm * TOP_K)

        def issue(t, c):
            for k in range(TOP_K):
                d = dest_ref[base + t * TOP_K + k]
                pltpu.make_async_copy(ys_ref.at[pl.ds(d, 1), :], buf.at[into, k, pl.ds(t, 1), :], sem.at[into]).start()
            return c

        lax.fori_loop(0, tm, issue, 0)

    @pl.when(i == 0)
    def _():
        gather(0, 0)

    @pl.when(i + 1 < pl.num_programs(0))
    def _():
        gather(i + 1, 1 - slot)

    for k in range(TOP_K):
        pltpu.make_async_copy(ys_ref.at[pl.ds(0, tm), :], buf.at[slot, k], sem.at[slot]).wait()

    gates = gate_ref[...]
    ffn = gates[:, 0:1] * buf[slot, 0]
    for k in range(1, TOP_K):
        ffn = ffn + gates[:, k:k + 1] * buf[slot, k]
    x2 = _layer_norm(alpha * x1_ref[...] + ffn, g_ref[...], b_ref[...])
    ple_gate = jax.nn.sigmoid(jnp.dot(x2.astype(BF16), pgw_ref[...], preferred_element_type=F32) + pgb_ref[...])
    emb = jnp.dot(p_ref[...].astype(BF16), pw_ref[...], preferred_element_type=F32)
    out = x2 + ple_gate * emb
    out_ref[...] = out
    if has_next:
        ob = out.astype(BF16)
        for c in range(0, proj_ref.shape[1], PROJ_CHUNK):
            proj_ref[:, c:c + PROJ_CHUNK] = jnp.dot(ob, w_next_ref[:, c:c + PROJ_CHUNK],
                                                    preferred_element_type=F32).astype(proj_ref.dtype)


def _combine(dest, ys, gates, x1, ln_g, ln_b, ple_gate_w, ple_gate_b, p, ple_w, w_next, alpha, tm=256):
    t, d = x1.shape
    dp = p.shape[1]
    tm = min(tm, t)
    tile = lambda i, dest: (i, 0)
    const = lambda i, dest: (0, 0)
    has_next = w_next is not None
    extra_in, extra_out_specs, extra_out_shapes = [], [], []
    if has_next:
        n = w_next.shape[1]
        assert n % PROJ_CHUNK == 0
        extra_in = [pl.BlockSpec((d, n), const, pipeline_mode=pl.Buffered(1))]
        extra_out_specs = [pl.BlockSpec((tm, n), tile)]
        extra_out_shapes = [jax.ShapeDtypeStruct((t, n), BF16)]
    outs = pl.pallas_call(
        functools.partial(_combine_kernel, alpha=alpha, has_next=has_next),
        grid_spec=pltpu.PrefetchScalarGridSpec(
            num_scalar_prefetch=1,
            grid=(t // tm,),
            in_specs=[
                pl.BlockSpec(memory_space=pl.ANY),
                pl.BlockSpec((tm, LANES), tile),
                pl.BlockSpec((tm, d), tile),
                pl.BlockSpec((1, d), const),
                pl.BlockSpec((1, d), const),
                pl.BlockSpec((d, d), const),
                pl.BlockSpec((1, d), const),
                pl.BlockSpec((tm, dp), tile),
                pl.BlockSpec((dp, d), const),
            ] + extra_in,
            out_specs=[pl.BlockSpec((tm, d), tile)] + extra_out_specs,
            scratch_shapes=[pltpu.VMEM((2, TOP_K, tm, d), F32), pltpu.SemaphoreType.DMA((2,))],
        ),
        out_shape=[jax.ShapeDtypeStruct((t, d), F32)] + extra_out_shapes,
        compiler_params=_cparams(("arbitrary",)),
        name="combine_ln_ple",
    )(dest, ys, gates, x1, ln_g.reshape(1, d), ln_b.reshape(1, d), ple_gate_w, ple_gate_b.reshape(1, d), p, ple_w,
      *([w_next] if has_next else []))
    return (outs[0], outs[1]) if has_next else (outs[0], None)


def _routing(idx_rank, counts, n_exp):
    top_idx = idx_rank[:, :TOP_K]
    rank = idx_rank[:, TOP_K:2 * TOP_K]
    counts = counts[0, :n_exp].astype(jnp.int32)
    padded = (counts + ROW_BLOCK - 1) // ROW_BLOCK * ROW_BLOCK
    pad_ends = jnp.cumsum(padded)
    pad_starts = pad_ends - padded
    experts = jnp.arange(n_exp, dtype=jnp.int32)
    start_of = jnp.sum(jnp.where(top_idx[:, :, None] == experts, pad_starts, 0), axis=-1)
    dest = (start_of + rank).reshape(-1).astype(jnp.int32)
    blocks = ((pad_starts // ROW_BLOCK).astype(jnp.int32), (padded // ROW_BLOCK).astype(jnp.int32))
    pad_rows = ((pad_starts + counts).astype(jnp.int32), pad_ends.astype(jnp.int32))
    return dest, blocks, pad_rows


def kernel(x, p, ret_w_in, ret_w_out, sb_w_in, sb_w_out, ln1_g, ln1_b, router_w, router_b, w_gate_up, b_gate_up,
           w_down, b_down, ln2_g, ln2_b, ple_w, ple_gate_w, ple_gate_b):
    batch, seq, d = x.shape
    depth = ln1_g.shape[0]
    n_exp = router_w.shape[-1]
    t = batch * seq
    alpha = float((2 * depth) ** 0.25)
    ret_heads = d // RET_QK_DIM
    n_blocks = -(-(t * TOP_K + n_exp * (ROW_BLOCK - 1)) // ROW_BLOCK)
    n_rows = n_blocks * ROW_BLOCK

    def w_in(layer):
        return (ret_w_in if layer % 2 == 0 else sb_w_in)[layer // 2].astype(BF16)

    xf = x.reshape(t, d)
    proj = _matmul(xf.astype(BF16), w_in(0), BF16)
    for i in range(depth):
        j = i // 2
        if i % 2 == 0:
            mixed = _retention(proj, batch, seq, ret_heads)
            w_out = ret_w_out[j].astype(BF16)
        else:
            mixed = _sb_attention(proj, batch, seq, d)
            w_out = sb_w_out[j].astype(BF16)
        x1, idx_rank, gates, counts = _mix_router(mixed, w_out, xf, ln1_g[i], ln1_b[i], router_w[i], router_b[i], alpha)
        dest, (block_first, block_count), (pad_lo, pad_hi) = _routing(idx_rank, counts, n_exp)
        xs = _dispatch(dest, pad_lo, pad_hi, x1, n_rows)
        ys = _experts(block_first, block_count, xs, i, w_gate_up, b_gate_up[i][:, 0::2], b_gate_up[i][:, 1::2],
                      w_down, b_down[i])
        xf, proj = _combine(dest, ys, gates, x1, ln2_g[i], ln2_b[i], ple_gate_w[i].astype(BF16), ple_gate_b[i],
                            p[i].reshape(t, -1), ple_w[i].astype(BF16), w_in(i + 1) if i + 1 < depth else None, alpha)
    return xf.reshape(batch, seq, d)
```

```python
import functools

import jax
import jax.numpy as jnp
import numpy as np
from jax import lax
from jax.experimental import pallas as pl
from jax.experimental.pallas import tpu as pltpu

F32 = jnp.float32
BF16 = jnp.bfloat16

RET_QK_DIM = 256
RET_V_DIM = 512
ROPE_BASE = 10000.0
GN_EPS = 1e-6
LN_EPS = 1e-5
SB_HEADS = 16
TOP_K = 4
SWIGLU_LIMIT = 7.0
SWIGLU_ALPHA = 1.702

LANES = 128
MXU_DIM = 256
VMEM_LIMIT = 60 * 1024 * 1024

RET_BLOCK = 512
RET_CHUNK = 64
SB_TILE = 256
ROW_BLOCK = 256
PROJ_CHUNK = 1024
EXP_ZERO_BELOW = -104.0


def _cparams(sem):
    return pltpu.CompilerParams(dimension_semantics=sem, vmem_limit_bytes=VMEM_LIMIT)


def _mm_kernel(x_ref, w_ref, o_ref):
    o_ref[...] = jnp.dot(x_ref[...], w_ref[...], preferred_element_type=F32).astype(o_ref.dtype)


def _matmul(x, w, out_dtype, tm=1024, tn=1024):
    m, k = x.shape
    n = w.shape[1]
    tm = min(tm, m)
    tn = min(tn, n)
    return pl.pallas_call(
        _mm_kernel,
        grid=(n // tn, m // tm),
        in_specs=[pl.BlockSpec((tm, k), lambda j, i: (i, 0)),
                  pl.BlockSpec((k, tn), lambda j, i: (0, j))],
        out_specs=pl.BlockSpec((tm, tn), lambda j, i: (i, j)),
        out_shape=jax.ShapeDtypeStruct((m, n), out_dtype),
        compiler_params=_cparams(("parallel", "parallel")),
        name="proj_matmul",
    )(x, w)


def _retention_kernel(q_ref, k_ref, v_ref, g_ref, cos_ref, sin_ref, dmat_ref, qdec_ref, kdec_ref,
                      cdec_ref, o_ref, state_ref):
    half = RET_QK_DIM // 2
    cos = cos_ref[...]
    sin = sin_ref[...]

    def rot(t):
        t1, t2 = t[:, :half], t[:, half:]
        return jnp.concatenate([t1 * cos - t2 * sin, t1 * sin + t2 * cos], axis=-1)

    q = rot(q_ref[...].astype(F32))
    k = rot(k_ref[...].astype(F32)) * (RET_QK_DIM ** -0.5)
    v = v_ref[...]

    @pl.when(pl.program_id(2) == 0)
    def _():
        state_ref[...] = jnp.zeros_like(state_ref)

    state = state_ref[...]
    scores = lax.dot_general(q.astype(BF16), k.astype(BF16), (((1,), (1,)), ((), ())),
                             preferred_element_type=F32) * dmat_ref[...]
    inner = jnp.dot(scores.astype(BF16), v, preferred_element_type=F32)
    cross = jnp.dot((q * qdec_ref[...]).astype(BF16), state.astype(BF16), preferred_element_type=F32)
    kv = lax.dot_general((k * kdec_ref[...]).astype(BF16), v, (((0,), (0,)), ((), ())),
                         preferred_element_type=F32)
    state_ref[...] = state * cdec_ref[...] + kv

    out = inner + cross
    mu = jnp.mean(out, axis=-1, keepdims=True)
    var = jnp.mean(jnp.square(out - mu), axis=-1, keepdims=True)
    normed = (out - mu) * lax.rsqrt(var + GN_EPS)
    g = g_ref[...].astype(F32)
    o_ref[...] = (g * jax.nn.sigmoid(g) * normed).astype(o_ref.dtype)


def _retention_tables(seq, heads):
    half = RET_QK_DIM // 2
    inv_freq = 1.0 / (ROPE_BASE ** (jnp.arange(half, dtype=F32) / half))
    ang = jnp.arange(seq, dtype=F32)[:, None] * inv_freq[None, :]
    log_gamma = jnp.log(1.0 - 2.0 ** (-5.0 - jnp.arange(heads, dtype=F32)))
    pos = jnp.arange(RET_BLOCK)
    diff = (pos[:, None] - pos[None, :]).astype(F32)
    same_chunk = (pos[:, None] // RET_CHUNK) == (pos[None, :] // RET_CHUNK)
    earlier_chunk = (pos[None, :] // RET_CHUNK) < (pos[:, None] // RET_CHUNK)
    lg = log_gamma[:, None, None]
    dmat = jnp.where(same_chunk[None], jnp.exp(lg * jnp.abs(diff)[None]),
                     jnp.where(earlier_chunk[None], jnp.exp(lg * diff[None]), 0.0))
    idx = pos.astype(F32)
    qdec = jnp.exp(log_gamma[:, None] * (idx + 1.0))[:, :, None]
    kdec = jnp.exp(log_gamma[:, None] * (RET_BLOCK - 1.0 - idx))[:, :, None]
    cdec = jnp.exp(log_gamma * RET_BLOCK)[:, None, None]
    return jnp.cos(ang), jnp.sin(ang), dmat, qdec, kdec, cdec


def _retention(proj, batch, seq, heads):
    t = proj.shape[0]
    blk = RET_BLOCK
    nblk = seq // blk
    dk, dv = RET_QK_DIM, RET_V_DIM
    cos, sin, dmat, qdec, kdec, cdec = _retention_tables(seq, heads)
    row = lambda b, h, i: b * nblk + i
    return pl.pallas_call(
        _retention_kernel,
        grid=(batch, heads, nblk),
        in_specs=[
            pl.BlockSpec((blk, dk), lambda b, h, i: (row(b, h, i), h)),
            pl.BlockSpec((blk, dk), lambda b, h, i: (row(b, h, i), heads + h)),
            pl.BlockSpec((blk, dv), lambda b, h, i: (row(b, h, i), heads + h)),
            pl.BlockSpec((blk, dv), lambda b, h, i: (row(b, h, i), 2 * heads + h)),
            pl.BlockSpec((blk, dk // 2), lambda b, h, i: (i, 0)),
            pl.BlockSpec((blk, dk // 2), lambda b, h, i: (i, 0)),
            pl.BlockSpec((None, blk, blk), lambda b, h, i: (h, 0, 0)),
            pl.BlockSpec((None, blk, 1), lambda b, h, i: (h, 0, 0)),
            pl.BlockSpec((None, blk, 1), lambda b, h, i: (h, 0, 0)),
            pl.BlockSpec((None, 1, 1), lambda b, h, i: (h, 0, 0)),
        ],
        out_specs=pl.BlockSpec((blk, dv), lambda b, h, i: (row(b, h, i), h)),
        out_shape=jax.ShapeDtypeStruct((t, heads * dv), BF16),
        scratch_shapes=[pltpu.VMEM((dk, dv), F32)],
        compiler_params=_cparams(("parallel", "parallel", "arbitrary")),
        name="retention",
    )(proj, proj, proj, proj, cos, sin, dmat, qdec, kdec, cdec)


def _sb_kernel(q_ref, k_ref, v_ref, u_ref, o_ref, *, head_dim):
    tq = SB_TILE
    n_q = q_ref.shape[0] // tq
    lane = lax.broadcasted_iota(jnp.int32, (tq, LANES), 1)
    first = lane < head_dim
    scale = jnp.asarray(head_dim ** -0.5, BF16)
    u = u_ref[...]
    row = lax.broadcasted_iota(jnp.int32, (tq, tq), 0)
    col = lax.broadcasted_iota(jnp.int32, (tq, tq), 1)
    past = col < row

    def tile(q_heads, j, carry, diagonal):
        run0, run1, acc = carry
        start = pl.multiple_of(j * tq, tq)
        kt = k_ref[pl.ds(start, tq), :]
        vt = v_ref[pl.ds(start, tq), :]
        new_runs = []
        pvs = []
        for qh, run in zip(q_heads, (run0, run1)):
            z = lax.dot_general(qh, kt, (((1,), (1,)), ((), ())), preferred_element_type=F32)
            sp = jnp.maximum(z, 0.0) + jnp.log(1.0 + jnp.exp(-jnp.abs(z)))
            masked = jnp.where(past, sp, 0.0) if diagonal else sp
            hi = masked.astype(BF16)
            lo = (masked - hi.astype(F32)).astype(BF16)
            cs = jnp.dot(jnp.concatenate([hi, lo], axis=1), u, preferred_element_type=F32)
            a = jnp.exp(z - sp - cs[:, :tq] - jnp.concatenate([run, run], axis=1))
            if diagonal:
                a = jnp.where(past, a, 0.0)
            pvs.append(jnp.dot(a.astype(BF16), vt, preferred_element_type=F32))
            new_runs.append(run + cs[:, tq:])
        acc = acc + jnp.where(first, pvs[0], pvs[1])
        return new_runs[0], new_runs[1], acc

    def q_tile(i, carry):
        rows = pl.ds(pl.multiple_of(i * tq, tq), tq)
        q = q_ref[rows, :] * scale
        zero = jnp.zeros_like(q)
        q_heads = (jnp.where(first, q, zero), jnp.where(first, zero, q))
        zeros = jnp.zeros((tq, LANES), F32)
        run0, run1, acc = tile(q_heads, i, (zeros, zeros, zeros), True)

        def cond(c):
            return jnp.logical_and(c[0] >= 0, c[1] <= -EXP_ZERO_BELOW)

        def body(c):
            j, _, r0, r1, ac = c
            r0, r1, ac = tile(q_heads, j, (r0, r1, ac), False)
            return j - 1, jnp.min(jnp.minimum(r0, r1)), r0, r1, ac

        init = (i - 1, jnp.min(jnp.minimum(run0, run1)), run0, run1, acc)
        acc = lax.while_loop(cond, body, init)[4]
        o_ref[rows, :] = acc.astype(o_ref.dtype)
        return carry

    lax.fori_loop(0, n_q, q_tile, 0)


def _sb_attention(proj, batch, seq, d_model):
    t = proj.shape[0]
    head_dim = d_model // SB_HEADS
    assert 2 * head_dim == LANES
    groups = d_model // LANES
    tq = SB_TILE
    j = np.arange(tq)
    strictly_later = (j[:, None] > j[None, :]).astype(np.float32)
    u1 = np.concatenate([strictly_later, np.ones((tq, LANES), np.float32)], axis=1)
    u = jnp.asarray(np.concatenate([u1, u1], axis=0), BF16)
    return pl.pallas_call(
        functools.partial(_sb_kernel, head_dim=head_dim),
        grid=(batch, groups),
        in_specs=[
            pl.BlockSpec((seq, LANES), lambda b, p: (b, p)),
            pl.BlockSpec((seq, LANES), lambda b, p: (b, groups + p)),
            pl.BlockSpec((seq, LANES), lambda b, p: (b, 2 * groups + p)),
            pl.BlockSpec((2 * tq, tq + LANES), lambda b, p: (0, 0)),
        ],
        out_specs=pl.BlockSpec((seq, LANES), lambda b, p: (b, p)),
        out_shape=jax.ShapeDtypeStruct((t, d_model), BF16),
        compiler_params=_cparams(("parallel", "parallel")),
        name="stick_breaking",
    )(proj, proj, proj, u)


def _layer_norm(h, g, b):
    mu = jnp.mean(h, axis=-1, keepdims=True)
    var = jnp.mean(jnp.square(h - mu), axis=-1, keepdims=True)
    return (h - mu) * lax.rsqrt(var + LN_EPS) * g + b


def _split_bf16(x):
    hi = x.astype(BF16)
    return hi, (x - hi.astype(F32)).astype(BF16)


def _mix_router_kernel(a_ref, w_ref, x_ref, g_ref, b_ref, rw_ref, rb_ref, tri_ref,
                       x1_ref, idx_ref, gate_ref, cnt_ref, carry_ref, *, alpha):
    tm = a_ref.shape[0]

    @pl.when(pl.program_id(0) == 0)
    def _():
        carry_ref[...] = jnp.zeros_like(carry_ref)

    y = jnp.dot(a_ref[...], w_ref[...], preferred_element_type=F32)
    x1 = _layer_norm(alpha * x_ref[...] + y, g_ref[...], b_ref[...])
    x1_ref[...] = x1

    xh, xl = _split_bf16(x1)
    wh, wl = _split_bf16(rw_ref[...])
    logits = (jnp.dot(xh, wh, preferred_element_type=F32) + jnp.dot(xl, wh, preferred_element_type=F32)
              + jnp.dot(xh, wl, preferred_element_type=F32)) + rb_ref[...]

    lane = lax.broadcasted_iota(jnp.int32, (tm, LANES), 1)
    work = logits
    sel_idx, sel_val, onehots = [], [], []
    for _ in range(TOP_K):
        m = jnp.max(work, axis=-1, keepdims=True)
        sel = jnp.min(jnp.where(work == m, lane, LANES), axis=-1, keepdims=True)
        hit = lane == sel
        sel_idx.append(sel)
        sel_val.append(m)
        onehots.append(hit)
        work = jnp.where(hit, -jnp.inf, work)
    exps = [jnp.exp(v - sel_val[0]) for v in sel_val]
    denom = exps[0] + exps[1] + exps[2] + exps[3]

    member = jnp.zeros((tm, LANES), F32)
    for hit in onehots:
        member = member + jnp.where(hit, 1.0, 0.0)
    prefix = jnp.dot(tri_ref[...], member.astype(BF16), preferred_element_type=F32)
    base = carry_ref[0:1, :] + prefix
    idx_out = jnp.zeros((tm, LANES), jnp.int32)
    gate_out = jnp.zeros((tm, LANES), F32)
    for k in range(TOP_K):
        rank = jnp.sum(jnp.where(onehots[k], base, 0.0), axis=-1, keepdims=True).astype(jnp.int32)
        idx_out = jnp.where(lane == k, sel_idx[k], idx_out)
        idx_out = jnp.where(lane == TOP_K + k, rank, idx_out)
        gate_out = jnp.where(lane == k, exps[k] / denom, gate_out)
    idx_ref[...] = idx_out
    gate_ref[...] = gate_out
    carry_ref[...] = carry_ref[...] + jnp.sum(member, axis=0, keepdims=True)
    cnt_ref[...] = carry_ref[...]


def _mix_router(a, w_out, x, ln_g, ln_b, router_w, router_b, alpha, tm=1024):
    t, kin = a.shape
    d = x.shape[1]
    e = router_w.shape[1]
    tm = min(tm, t)
    rw = jnp.pad(router_w, ((0, 0), (0, LANES - e)))
    rb = jnp.pad(router_b, (0, LANES - e), constant_values=-jnp.inf).reshape(1, LANES)
    r = np.arange(tm)
    tri = jnp.asarray((r[None, :] < r[:, None]).astype(np.float32), BF16)
    tile = lambda i: (i, 0)
    const = lambda i: (0, 0)
    return pl.pallas_call(
        functools.partial(_mix_router_kernel, alpha=alpha),
        grid=(t // tm,),
        in_specs=[
            pl.BlockSpec((tm, kin), tile),
            pl.BlockSpec((kin, d), const),
            pl.BlockSpec((tm, d), tile),
            pl.BlockSpec((1, d), const),
            pl.BlockSpec((1, d), const),
            pl.BlockSpec((d, LANES), const),
            pl.BlockSpec((1, LANES), const),
            pl.BlockSpec((tm, tm), const),
        ],
        out_specs=[
            pl.BlockSpec((tm, d), tile),
            pl.BlockSpec((tm, LANES), tile),
            pl.BlockSpec((tm, LANES), tile),
            pl.BlockSpec((8, LANES), const),
        ],
        out_shape=[
            jax.ShapeDtypeStruct((t, d), F32),
            jax.ShapeDtypeStruct((t, LANES), jnp.int32),
            jax.ShapeDtypeStruct((t, LANES), F32),
            jax.ShapeDtypeStruct((8, LANES), F32),
        ],
        scratch_shapes=[pltpu.VMEM((8, LANES), F32)],
        compiler_params=_cparams(("arbitrary",)),
        name="mix_ln_router",
    )(a, w_out, x, ln_g.reshape(1, d), ln_b.reshape(1, d), rw, rb, tri)


def _dispatch_kernel(dest_ref, pad_lo_ref, pad_hi_ref, x_ref, xs_ref, zrow, sem, zsem):
    tm = x_ref.shape[0]
    base = pl.program_id(0) * (tm * TOP_K)

    def row_copy(t, d):
        return pltpu.make_async_copy(x_ref.at[pl.ds(t, 1), :], xs_ref.at[pl.ds(d, 1), :], sem)

    def issue(t, c):
        for k in range(TOP_K):
            row_copy(t, dest_ref[base + t * TOP_K + k]).start()
        return c

    lax.fori_loop(0, tm, issue, 0)

    @pl.when(pl.program_id(0) == pl.num_programs(0) - 1)
    def _():
        zrow[...] = jnp.zeros_like(zrow)
        n_exp = pad_lo_ref.shape[0]
        blk = zrow.shape[0]
        first_free = pad_hi_ref[n_exp - 1] // blk
        n_blk = xs_ref.shape[0] // blk

        def zero_row(r):
            return pltpu.make_async_copy(zrow.at[pl.ds(0, 1), :], xs_ref.at[pl.ds(r, 1), :], zsem)

        def zero_block(j):
            return pltpu.make_async_copy(zrow, xs_ref.at[pl.ds(pl.multiple_of(j * blk, blk), blk), :], zsem)

        def start_rows(e, c):
            return lax.fori_loop(pad_lo_ref[e], pad_hi_ref[e], lambda r, c2: (zero_row(r).start(), c2)[1], c)

        def wait_rows(e, c):
            return lax.fori_loop(pad_lo_ref[e], pad_hi_ref[e], lambda r, c2: (zero_row(r).wait(), c2)[1], c)

        lax.fori_loop(0, n_exp, start_rows, 0)
        lax.fori_loop(first_free, n_blk, lambda j, c: (zero_block(j).start(), c)[1], 0)
        lax.fori_loop(0, n_exp, wait_rows, 0)
        lax.fori_loop(first_free, n_blk, lambda j, c: (zero_block(j).wait(), c)[1], 0)

    for k in range(TOP_K):
        pltpu.make_async_copy(x_ref, xs_ref.at[pl.ds(0, tm), :], sem).wait()


def _dispatch(dest, pad_lo, pad_hi, x1, n_rows, tm=512):
    t, d = x1.shape
    tm = min(tm, t)
    return pl.pallas_call(
        _dispatch_kernel,
        grid_spec=pltpu.PrefetchScalarGridSpec(
            num_scalar_prefetch=3,
            grid=(t // tm,),
            in_specs=[pl.BlockSpec((tm, d), lambda i, *_: (i, 0))],
            out_specs=pl.BlockSpec(memory_space=pl.ANY),
            scratch_shapes=[pltpu.VMEM((ROW_BLOCK, d), x1.dtype), pltpu.SemaphoreType.DMA, pltpu.SemaphoreType.DMA],
        ),
        out_shape=jax.ShapeDtypeStruct((n_rows, d), x1.dtype),
        compiler_params=_cparams(("arbitrary",)),
        name="dispatch_rows",
    )(dest, pad_lo, pad_hi, x1)


def _experts_kernel(bstart_ref, bcount_ref, xs_ref, wgu_ref, bg_ref, bu_ref, wd_ref, bd_ref, perm_ref,
                    ys_ref, wg_s, wu_s, wd_s, xbuf, ybuf, xtail, ytail, xsem, ysem, tsem, *, n_blocks):
    e = pl.program_id(0)
    first = bstart_ref[e]
    n = bcount_ref[e]
    tm = ROW_BLOCK
    big = 2 * tm
    f = wd_ref.shape[0]
    n_big = n // 2
    has_tail = n % 2 == 1

    def big_rows(j):
        return pl.ds(pl.multiple_of((first + 2 * j) * tm, tm), big)

    def x_copy(j, slot):
        return pltpu.make_async_copy(xs_ref.at[big_rows(j), :], xbuf.at[slot], xsem.at[slot])

    def y_copy(j, slot):
        return pltpu.make_async_copy(ybuf.at[slot], ys_ref.at[big_rows(j), :], ysem.at[slot])

    tail_rows = pl.ds(pl.multiple_of((first + n - 1) * tm, tm), tm)
    xt_copy = pltpu.make_async_copy(xs_ref.at[tail_rows, :], xtail, tsem.at[0])
    yt_copy = pltpu.make_async_copy(ytail, ys_ref.at[tail_rows, :], tsem.at[1])

    def ffn(x):
        x = x.astype(BF16)
        gate = jnp.dot(x, wg_s[...], preferred_element_type=F32) + bg_ref[...]
        up = jnp.dot(x, wu_s[...], preferred_element_type=F32) + bu_ref[...]
        gate = jnp.minimum(gate, SWIGLU_LIMIT)
        up = jnp.clip(up, -SWIGLU_LIMIT, SWIGLU_LIMIT)
        act = (up + 1.0) * (gate * jax.nn.sigmoid(gate * SWIGLU_ALPHA))
        return jnp.dot(act.astype(BF16), wd_s[...], preferred_element_type=F32) + bd_ref[...]

    @pl.when(n > 0)
    def _():
        @pl.when(n_big > 0)
        def _():
            x_copy(0, 0).start()

        @pl.when(has_tail)
        def _():
            xt_copy.start()

        half = MXU_DIM // 2
        for c in range(2 * f // MXU_DIM):
            blk = wgu_ref[:, c * MXU_DIM:(c + 1) * MXU_DIM].astype(BF16)
            sep = jnp.dot(blk, perm_ref[...], preferred_element_type=F32).astype(BF16)
            wg_s[:, c * half:(c + 1) * half] = sep[:, :half]
            wu_s[:, c * half:(c + 1) * half] = sep[:, half:]
        wd_s[...] = wd_ref[...].astype(BF16)

        def step(j, carry):
            slot = j % 2
            x_copy(j, slot).wait()

            @pl.when(j + 1 < n_big)
            def _():
                x_copy(j + 1, 1 - slot).start()

            @pl.when(j >= 2)
            def _():
                y_copy(j - 2, slot).wait()

            ybuf[slot] = ffn(xbuf[slot])
            y_copy(j, slot).start()
            return carry

        lax.fori_loop(0, n_big, step, 0)

        @pl.when(has_tail)
        def _():
            xt_copy.wait()
            ytail[...] = ffn(xtail[...])
            yt_copy.start()

        @pl.when(n_big >= 2)
        def _():
            y_copy(n_big - 2, n_big % 2).wait()

        @pl.when(n_big >= 1)
        def _():
            y_copy(n_big - 1, (n_big - 1) % 2).wait()

        @pl.when(has_tail)
        def _():
            yt_copy.wait()

    @pl.when(e == pl.num_programs(0) - 1)
    def _():
        used = first + n

        @pl.when(used < n_blocks)
        def _():
            ytail[...] = jnp.zeros_like(ytail)

            def fill(j, carry):
                cp = pltpu.make_async_copy(ytail, ys_ref.at[pl.ds(pl.multiple_of(j * tm, tm), tm), :], tsem.at[1])
                cp.start()
                cp.wait()
                return carry

            lax.fori_loop(used, n_blocks, fill, 0)


def _experts(block_first, block_count, xs, layer, w_gate_up, b_gate, b_up, w_down, b_down):
    n_rows, d = xs.shape
    _, n_exp, _, f2 = w_gate_up.shape
    f = f2 // 2
    tm = ROW_BLOCK
    c = np.arange(MXU_DIM)
    src = np.where(c < MXU_DIM // 2, 2 * c, 2 * (c - MXU_DIM // 2) + 1)
    perm = jnp.asarray((np.arange(MXU_DIM)[:, None] == src[None, :]).astype(np.float32), BF16)
    by_expert = lambda e, bf, bc: (e, 0, 0)
    by_layer_expert = lambda e, bf, bc: (layer, e, 0, 0)
    return pl.pallas_call(
        functools.partial(_experts_kernel, n_blocks=n_rows // tm),
        grid_spec=pltpu.PrefetchScalarGridSpec(
            num_scalar_prefetch=2,
            grid=(n_exp,),
            in_specs=[
                pl.BlockSpec(memory_space=pl.ANY),
                pl.BlockSpec((None, None, d, f2), by_layer_expert),
                pl.BlockSpec((None, 1, f), by_expert),
                pl.BlockSpec((None, 1, f), by_expert),
                pl.BlockSpec((None, None, f, d), by_layer_expert),
                pl.BlockSpec((None, 1, d), by_expert),
                pl.BlockSpec((MXU_DIM, MXU_DIM), lambda e, bf, bc: (0, 0)),
            ],
            out_specs=pl.BlockSpec(memory_space=pl.ANY),
            scratch_shapes=[pltpu.VMEM((d, f), BF16), pltpu.VMEM((d, f), BF16), pltpu.VMEM((f, d), BF16),
                            pltpu.VMEM((2, 2 * tm, d), F32), pltpu.VMEM((2, 2 * tm, d), F32),
                            pltpu.VMEM((tm, d), F32), pltpu.VMEM((tm, d), F32),
                            pltpu.SemaphoreType.DMA((2,)), pltpu.SemaphoreType.DMA((2,)), pltpu.SemaphoreType.DMA((2,))],
        ),
        out_shape=jax.ShapeDtypeStruct((n_rows, d), F32),
        compiler_params=_cparams(("arbitrary",)),
        name="expert_swiglu",
    )(block_first, block_count, xs, w_gate_up, b_gate.reshape(n_exp, 1, f), b_up.reshape(n_exp, 1, f),
      w_down, b_down.reshape(n_exp, 1, d), perm)


def _combine_kernel(dest_ref, ys_ref, gate_ref, x1_ref, g_ref, b_ref, pgw_ref, pgb_ref, p_ref, pw_ref, *rest,
                    alpha, has_next):
    if has_next:
        w_next_ref, out_ref, proj_ref, buf, sem = rest
    else:
        out_ref, buf, sem = rest
    tm = x1_ref.shape[0]
    i = pl.program_id(0)
    slot = i % 2

    def gather(tile, into):
        base = tile * (tm * TOP_K)

        def issue(t, c):
            for k in range(TOP_K):
                d = dest_ref[base + t * TOP_K + k]
                pltpu.make_async_copy(ys_ref.at[pl.ds(d, 1), :], buf.at[into, k, pl.ds(t, 1), :], sem.at[into]).start()
            return c

        lax.fori_loop(0, tm, issue, 0)

    @pl.when(i == 0)
    def _():
        gather(0, 0)

    @pl.when(i + 1 < pl.num_programs(0))
    def _():
        gather(i + 1, 1 - slot)

    for k in range(TOP_K):
        pltpu.make_async_copy(ys_ref.at[pl.ds(0, tm), :], buf.at[slot, k], sem.at[slot]).wait()

    gates = gate_ref[...]
    ffn = gates[:, 0:1] * buf[slot, 0]
    for k in range(1, TOP_K):
        ffn = ffn + gates[:, k:k + 1] * buf[slot, k]
    x2 = _layer_norm(alpha * x1_ref[...] + ffn, g_ref[...], b_ref[...])
    ple_gate = jax.nn.sigmoid(jnp.dot(x2.astype(BF16), pgw_ref[...], preferred_element_type=F32) + pgb_ref[...])
    emb = jnp.dot(p_ref[...].astype(BF16), pw_ref[...], preferred_element_type=F32)
    out = x2 + ple_gate * emb
    out_ref[...] = out
    if has_next:
        ob = out.astype(BF16)
        for c in range(0, proj_ref.shape[1], PROJ_CHUNK):
            proj_ref[:, c:c + PROJ_CHUNK] = jnp.dot(ob, w_next_ref[:, c:c + PROJ_CHUNK],
                                                    preferred_element_type=F32).astype(proj_ref.dtype)


def _combine(dest, ys, gates, x1, ln_g, ln_b, ple_gate_w, ple_gate_b, p, ple_w, w_next, alpha, tm=256):
    t, d = x1.shape
    dp = p.shape[1]
    tm = min(tm, t)
    tile = lambda i, dest: (i, 0)
    const = lambda i, dest: (0, 0)
    has_next = w_next is not None
    extra_in, extra_out_specs, extra_out_shapes = [], [], []
    if has_next:
        n = w_next.shape[1]
        assert n % PROJ_CHUNK == 0
        extra_in = [pl.BlockSpec((d, n), const, pipeline_mode=pl.Buffered(1))]
        extra_out_specs = [pl.BlockSpec((tm, n), tile)]
        extra_out_shapes = [jax.ShapeDtypeStruct((t, n), BF16)]
    outs = pl.pallas_call(
        functools.partial(_combine_kernel, alpha=alpha, has_next=has_next),
        grid_spec=pltpu.PrefetchScalarGridSpec(
            num_scalar_prefetch=1,
            grid=(t // tm,),
            in_specs=[
                pl.BlockSpec(memory_space=pl.ANY),
                pl.BlockSpec((tm, LANES), tile),
                pl.BlockSpec((tm, d), tile),
                pl.BlockSpec((1, d), const),
                pl.BlockSpec((1, d), const),
                pl.BlockSpec((d, d), const),
                pl.BlockSpec((1, d), const),
                pl.BlockSpec((tm, dp), tile),
                pl.BlockSpec((dp, d), const),
            ] + extra_in,
            out_specs=[pl.BlockSpec((tm, d), tile)] + extra_out_specs,
            scratch_shapes=[pltpu.VMEM((2, TOP_K, tm, d), F32), pltpu.SemaphoreType.DMA((2,))],
        ),
        out_shape=[jax.ShapeDtypeStruct((t, d), F32)] + extra_out_shapes,
        compiler_params=_cparams(("arbitrary",)),
        name="combine_ln_ple",
    )(dest, ys, gates, x1, ln_g.reshape(1, d), ln_b.reshape(1, d), ple_gate_w, ple_gate_b.reshape(1, d), p, ple_w,
      *([w_next] if has_next else []))
    return (outs[0], outs[1]) if has_next else (outs[0], None)


def _routing(idx_rank, counts, n_exp):
    top_idx = idx_rank[:, :TOP_K]
    rank = idx_rank[:, TOP_K:2 * TOP_K]
    counts = counts[0, :n_exp].astype(jnp.int32)
    padded = (counts + ROW_BLOCK - 1) // ROW_BLOCK * ROW_BLOCK
    pad_ends = jnp.cumsum(padded)
    pad_starts = pad_ends - padded
    experts = jnp.arange(n_exp, dtype=jnp.int32)
    start_of = jnp.sum(jnp.where(top_idx[:, :, None] == experts, pad_starts, 0), axis=-1)
    dest = (start_of + rank).reshape(-1).astype(jnp.int32)
    blocks = ((pad_starts // ROW_BLOCK).astype(jnp.int32), (padded // ROW_BLOCK).astype(jnp.int32))
    pad_rows = ((pad_starts + counts).astype(jnp.int32), pad_ends.astype(jnp.int32))
    return dest, blocks, pad_rows


def kernel(x, p, ret_w_in, ret_w_out, sb_w_in, sb_w_out, ln1_g, ln1_b, router_w, router_b, w_gate_up, b_gate_up,
           w_down, b_down, ln2_g, ln2_b, ple_w, ple_gate_w, ple_gate_b):
    batch, seq, d = x.shape
    depth = ln1_g.shape[0]
    n_exp = router_w.shape[-1]
    t = batch * seq
    alpha = float((2 * depth) ** 0.25)
    ret_heads = d // RET_QK_DIM
    n_blocks = -(-(t * TOP_K + n_exp * (ROW_BLOCK - 1)) // ROW_BLOCK)
    n_rows = n_blocks * ROW_BLOCK

    def w_in(layer):
        return (ret_w_in if layer % 2 == 0 else sb_w_in)[layer // 2].astype(BF16)

    xf = x.reshape(t, d)
    proj = _matmul(xf.astype(BF16), w_in(0), BF16)
    for i in range(depth):
        j = i // 2
        if i % 2 == 0:
            mixed = _retention(proj, batch, seq, ret_heads)
            w_out = ret_w_out[j].astype(BF16)
        else:
            mixed = _sb_attention(proj, batch, seq, d)
            w_out = sb_w_out[j].astype(BF16)
        x1, idx_rank, gates, counts = _mix_router(mixed, w_out, xf, ln1_g[i], ln1_b[i], router_w[i], router_b[i], alpha)
        dest, (block_first, block_count), (pad_lo, pad_hi) = _routing(idx_rank, counts, n_exp)
        xs = _dispatch(dest, pad_lo, pad_hi, x1, n_rows)
        ys = _experts(block_first, block_count, xs, i, w_gate_up, b_gate_up[i][:, 0::2], b_gate_up[i][:, 1::2],
                      w_down, b_down[i])
        xf, proj = _combine(dest, ys, gates, x1, ln2_g[i], ln2_b[i], ple_gate_w[i].astype(BF16), ple_gate_b[i],
                            p[i].reshape(t, -1), ple_w[i].astype(BF16), w_in(i + 1) if i + 1 < depth else None, alpha)
    return xf.reshape(batch, seq, d)
```
